```python
import jax, jax.numpy as jnp
from jax import lax
import numpy as np

D_MODEL = 1024
BATCH = 4
SEQ = 8192
DEPTH = 2

N_MEM = 256
HEAD_DIM = 64
N_Q_HEADS = 8
N_KV_HEADS = 2
Q_PER_KV = N_Q_HEADS // N_KV_HEADS
ATTN_WIDTH = N_Q_HEADS * HEAD_DIM
KV_WIDTH = N_KV_HEADS * HEAD_DIM
WINDOW = 128
BLOCK = 128
ROPE_THETA = 10000.0
POOL_WINDOWS = (2, 4, 8, 16)
N_POOL_GROUPS = len(POOL_WINDOWS)
POOL_WIDTH = D_MODEL - ATTN_WIDTH
POOL_GROUP_DIM = POOL_WIDTH // N_POOL_GROUPS
MIX_WIDTH = ATTN_WIDTH + POOL_WIDTH
IN_WIDTH = ATTN_WIDTH + 2 * KV_WIDTH + POOL_WIDTH
X_HEADS = 4
X_HEAD_DIM = D_MODEL // X_HEADS
X_WIDTH = X_HEADS * X_HEAD_DIM
D_FF = 2816
FFN_RES = 0.5
EPS = 1e-6
MAX_POS_OFFSET = 1024
NEG = -1e30

kernel_name = "hymba_swa_sink_pool_macaron_xattn"


def rms_norm(x, g):
    xf = x.astype(jnp.float32)
    y = xf * lax.rsqrt(jnp.mean(xf * xf, axis=-1, keepdims=True) + EPS)
    return (y * g.astype(jnp.float32)).astype(x.dtype)


def swiglu(h, w_gate, w_up, w_down):
    return (jax.nn.silu(h @ w_gate) * (h @ w_up)) @ w_down


def rope_tables(positions):
    inv_freq = ROPE_THETA ** (-jnp.arange(0, HEAD_DIM, 2, dtype=jnp.float32) / HEAD_DIM)
    ang = positions.astype(jnp.float32)[..., None] * inv_freq
    return jnp.cos(ang)[:, :, None, :], jnp.sin(ang)[:, :, None, :]


def apply_rope(t, cos, sin):
    tf = t.astype(jnp.float32)
    t1, t2 = tf[..., : HEAD_DIM // 2], tf[..., HEAD_DIM // 2:]
    return jnp.concatenate([t1 * cos - t2 * sin, t2 * cos + t1 * sin], axis=-1).astype(t.dtype)


def sliding_window_sink_attention(q, k, v, sinks):
    B, S = q.shape[0], q.shape[1]
    nb = S // BLOCK
    qb = q.reshape(B, nb, BLOCK, N_KV_HEADS, Q_PER_KV, HEAD_DIM)

    def with_prev(t):
        tb = t.reshape(B, nb, BLOCK, N_KV_HEADS, HEAD_DIM)
        prev = jnp.pad(tb[:, :-1], ((0, 0), (1, 0), (0, 0), (0, 0), (0, 0)))
        return jnp.concatenate([prev, tb], axis=2)

    kk, vv = with_prev(k), with_prev(v)
    scale = HEAD_DIM ** -0.5
    s = jnp.einsum('bnqkgd,bnjkd->bkgnqj', qb, kk).astype(jnp.float32) * scale
    qi = jnp.arange(BLOCK)[:, None]
    kj = jnp.arange(2 * BLOCK)[None, :]
    diff = qi + BLOCK - kj
    blk = jnp.arange(nb)[:, None, None]
    key_abs = (blk - 1) * BLOCK + kj[None]
    mask = (diff >= 0)[None] & (diff < WINDOW)[None] & (key_abs >= 0)
    s = jnp.where(mask, s, NEG)
    sink = sinks.astype(jnp.float32).reshape(1, N_KV_HEADS, Q_PER_KV, 1, 1, 1)
    sink = jnp.broadcast_to(sink, s.shape[:-1] + (1,))
    p = jax.nn.softmax(jnp.concatenate([s, sink], axis=-1), axis=-1)[..., :-1]
    o = jnp.einsum('bkgnqj,bnjkd->bnqkgd', p.astype(v.dtype), vv)
    return o.reshape(B, S, ATTN_WIDTH)


def multiscale_pool(u, pool_w, pool_scale):
    B, S = u.shape[0], u.shape[1]
    ug = u.reshape(B, S, N_POOL_GROUPS, POOL_GROUP_DIM)
    uf = ug.astype(jnp.float32)
    c = jnp.pad(jnp.cumsum(uf, axis=1), ((0, 0), (1, 0), (0, 0), (0, 0)))
    t = jnp.arange(S)[:, None]
    w = jnp.array(POOL_WINDOWS, dtype=jnp.int32)[None, :]
    start = jnp.maximum(t + 1 - w, 0)
    g = jnp.arange(N_POOL_GROUPS)[None, :]
    window_sum = c[:, 1:] - c[:, start, g]
    count = jnp.minimum(t + 1, w).astype(jnp.float32)[None, :, :, None]
    pooled = (window_sum / count - uf).astype(u.dtype)
    mixed = jnp.einsum('bsgc,gcd->bsgd', pooled, pool_w)
    mixed = mixed * pool_scale.reshape(N_POOL_GROUPS, POOL_GROUP_DIM)
    return mixed.reshape(B, S, POOL_WIDTH)


def memory_cross_attention(h, mem_n, wq, wkv, wo):
    B, S = h.shape[0], h.shape[1]
    q = (h @ wq).reshape(B, S, X_HEADS, X_HEAD_DIM)
    kv = (mem_n @ wkv).reshape(B, mem_n.shape[1], 2, X_HEADS, X_HEAD_DIM)
    k, v = kv[:, :, 0], kv[:, :, 1]
    s = jnp.einsum('bshd,bmhd->bhsm', q, k).astype(jnp.float32) * (X_HEAD_DIM ** -0.5)
    p = jax.nn.softmax(s, axis=-1)
    o = jnp.einsum('bhsm,bmhd->bshd', p.astype(v.dtype), v).reshape(B, S, X_WIDTH)
    return o @ wo


def setup_inputs(seed: int = 0) -> dict:
    key = jax.random.key(seed)
    ks = jax.random.split(key, 32)
    f32 = jnp.float32
    L, D = DEPTH, D_MODEL

    def w(k, shape, fan_in):
        return jax.random.normal(k, shape, f32) * (fan_in ** -0.5)

    def gain(k, shape):
        return 1.0 + 0.05 * jax.random.normal(k, shape, f32)

    offset = jax.random.randint(ks[2], (BATCH, 1), 0, MAX_POS_OFFSET, dtype=jnp.int32)
    positions = offset + jnp.arange(SEQ, dtype=jnp.int32)[None, :]
    return {
        "x": jax.random.normal(ks[0], (BATCH, SEQ, D), f32),
        "mem": jax.random.normal(ks[1], (BATCH, N_MEM, D), f32),
        "positions": positions,
        "ffn1_norm": gain(ks[3], (L, D)),
        "ffn1_w_gate": w(ks[4], (L, D, D_FF), D),
        "ffn1_w_up": w(ks[5], (L, D, D_FF), D),
        "ffn1_w_down": w(ks[6], (L, D_FF, D), D_FF),
        "mix_norm": gain(ks[7], (L, D)),
        "w_in": w(ks[8], (L, D, IN_WIDTH), D),
        "attn_sinks": 0.5 * jax.random.normal(ks[9], (L, N_Q_HEADS), f32),
        "pool_w": w(ks[10], (L, N_POOL_GROUPS, POOL_GROUP_DIM, POOL_GROUP_DIM), POOL_GROUP_DIM),
        "pool_scale": gain(ks[11], (L, POOL_WIDTH)),
        "attn_out_norm": gain(ks[12], (L, ATTN_WIDTH)),
        "pool_out_norm": gain(ks[13], (L, POOL_WIDTH)),
        "w_out": w(ks[14], (L, MIX_WIDTH, D), MIX_WIDTH),
        "xattn_norm": gain(ks[15], (L, D)),
        "mem_norm": gain(ks[16], (L, D)),
        "xattn_wq": w(ks[17], (L, D, X_WIDTH), D),
        "xattn_wkv": w(ks[18], (L, D, 2 * X_WIDTH), D),
        "xattn_wo": w(ks[19], (L, X_WIDTH, D), X_WIDTH),
        "ffn2_norm": gain(ks[20], (L, D)),
        "ffn2_w_gate": w(ks[21], (L, D, D_FF), D),
        "ffn2_w_up": w(ks[22], (L, D, D_FF), D),
        "ffn2_w_down": w(ks[23], (L, D_FF, D), D_FF),
        "final_norm": gain(ks[24], (D,)),
    }


def reference(x, mem, positions, ffn1_norm, ffn1_w_gate, ffn1_w_up, ffn1_w_down,
              mix_norm, w_in, attn_sinks, pool_w, pool_scale, attn_out_norm, pool_out_norm,
              w_out, xattn_norm, mem_norm, xattn_wq, xattn_wkv, xattn_wo,
              ffn2_norm, ffn2_w_gate, ffn2_w_up, ffn2_w_down, final_norm):
    B, S = x.shape[0], x.shape[1]
    cos, sin = rope_tables(positions)
    for l in range(DEPTH):
        x = x + FFN_RES * swiglu(rms_norm(x, ffn1_norm[l]), ffn1_w_gate[l], ffn1_w_up[l], ffn1_w_down[l])

        h = rms_norm(x, mix_norm[l])
        proj = h @ w_in[l]
        q = proj[..., :ATTN_WIDTH].reshape(B, S, N_Q_HEADS, HEAD_DIM)
        k = proj[..., ATTN_WIDTH:ATTN_WIDTH + KV_WIDTH].reshape(B, S, N_KV_HEADS, HEAD_DIM)
        v = proj[..., ATTN_WIDTH + KV_WIDTH:ATTN_WIDTH + 2 * KV_WIDTH].reshape(B, S, N_KV_HEADS, HEAD_DIM)
        u = proj[..., ATTN_WIDTH + 2 * KV_WIDTH:]
        q = apply_rope(q, cos, sin)
        k = apply_rope(k, cos, sin)
        out_a = sliding_window_sink_attention(q, k, v, attn_sinks[l])
        out_b = multiscale_pool(u, pool_w[l], pool_scale[l])
        merged = jnp.concatenate([rms_norm(out_a, attn_out_norm[l]),
                                  rms_norm(out_b, pool_out_norm[l])], axis=-1)
        x = x + merged @ w_out[l]

        x = x + memory_cross_attention(rms_norm(x, xattn_norm[l]), rms_norm(mem, mem_norm[l]),
                                       xattn_wq[l], xattn_wkv[l], xattn_wo[l])

        x = x + FFN_RES * swiglu(rms_norm(x, ffn2_norm[l]), ffn2_w_gate[l], ffn2_w_up[l], ffn2_w_down[l])
    return rms_norm(x, final_norm)
```

```python
import functools

import jax
import jax.numpy as jnp
from jax import lax
from jax.experimental import pallas as pl
from jax.experimental.pallas import tpu as pltpu

F32 = jnp.float32
BF16 = jnp.bfloat16

D_MODEL = 1024
HEAD_DIM = 64
N_Q_HEADS = 8
N_KV_HEADS = 2
ATTN_WIDTH = N_Q_HEADS * HEAD_DIM
KV_WIDTH = N_KV_HEADS * HEAD_DIM
BLOCK = 128
POOL_WINDOWS = (2, 4, 8, 16)
POOL_GROUP_DIM = 128
POOL_WIDTH = 512
POOL_HALO = 16
X_HEADS = 4
X_HEAD_DIM = 256
D_FF = 2816
FFN_RES = 0.5
EPS = 1e-6
ROPE_THETA = 10000.0
NEG = -1e30
LANES = 128

ROW_TILE = 512
VMEM_LIMIT = 56 * 1024 * 1024


def _rms(x, g):
    ms = jnp.mean(x * x, axis=-1, keepdims=True)
    return x * lax.rsqrt(ms + EPS) * g


def _resident(shape):
    nd = len(shape)
    return pl.BlockSpec(shape, lambda *_: (0,) * nd, pipeline_mode=pl.Buffered(1))


def _rope_table_kernel(pos_ref, freq_ref, cos_ref, sin_ref):
    ang = pos_ref[...].astype(F32) * freq_ref[...]
    lane = lax.broadcasted_iota(jnp.int32, ang.shape, 1)
    first_half = (lane % HEAD_DIM) < (HEAD_DIM // 2)
    cos_ref[...] = jnp.cos(ang)
    s = jnp.sin(ang)
    sin_ref[...] = jnp.where(first_half, -s, s)


def _rope_tables(positions):
    n = positions.size
    inv_freq = ROPE_THETA ** (-jnp.arange(0, HEAD_DIM, 2, dtype=F32) / HEAD_DIM)
    freq = jnp.tile(inv_freq, LANES // (HEAD_DIM // 2)).reshape(1, LANES)
    tm = 2048
    return pl.pallas_call(
        _rope_table_kernel,
        grid=(n // tm,),
        in_specs=[pl.BlockSpec((tm, 1), lambda i: (i, 0)),
                  pl.BlockSpec((1, LANES), lambda i: (0, 0))],
        out_specs=[pl.BlockSpec((tm, LANES), lambda i: (i, 0))] * 2,
        out_shape=[jax.ShapeDtypeStruct((n, LANES), F32)] * 2,
        name="rope_tables",
    )(positions.reshape(n, 1), freq)


def _ffn_kernel(x_ref, g_ref, wg_ref, wu_ref, wd_ref, fg_ref, o_ref, *, final):
    x = x_ref[...]
    xn = _rms(x, g_ref[...]).astype(BF16)
    gate = jnp.dot(xn, wg_ref[...], preferred_element_type=F32)
    up = jnp.dot(xn, wu_ref[...], preferred_element_type=F32)
    h = (gate / (1.0 + jnp.exp(-gate)) * up).astype(BF16)
    y = jnp.dot(h, wd_ref[...], preferred_element_type=F32)
    out = x + FFN_RES * y
    if final:
        out = _rms(out, fg_ref[...])
    o_ref[...] = out


def _ffn(x, g, wg, wu, wd, fg, final):
    n, d = x.shape
    tm = ROW_TILE
    return pl.pallas_call(
        functools.partial(_ffn_kernel, final=final),
        grid=(n // tm,),
        in_specs=[pl.BlockSpec((tm, d), lambda i: (i, 0)),
                  _resident((1, d)), _resident(wg.shape), _resident(wu.shape),
                  _resident(wd.shape), _resident((1, d))],
        out_specs=pl.BlockSpec((tm, d), lambda i: (i, 0)),
        out_shape=jax.ShapeDtypeStruct((n, d), F32),
        compiler_params=pltpu.CompilerParams(
            dimension_semantics=("arbitrary",), vmem_limit_bytes=VMEM_LIMIT),
        name="ffn_final" if final else "ffn",
    )(x, g, wg, wu, wd, fg)


def _rope(t, cos, sin_signed):
    lane = lax.broadcasted_iota(jnp.int32, t.shape, 1)
    first_half = (lane % HEAD_DIM) < (HEAD_DIM // 2)
    swapped = jnp.where(first_half,
                        pltpu.roll(t, LANES - HEAD_DIM // 2, 1),
                        pltpu.roll(t, HEAD_DIM // 2, 1))
    return t * cos + swapped * sin_signed


def _mixer_kernel(sinks_ref, x_ref, cos_ref, sin_ref, g_ref, win_ref, pw_ref, ps_ref,
                  an_ref, pn_ref, wout_ref, o_ref,
                  kext, vext, ubuf, oa_ref, ob_ref):
    j = pl.program_id(1)
    tm = x_ref.shape[0]
    nblk = tm // BLOCK

    @pl.when(j == 0)
    def _():
        kext[:, 0:BLOCK, :] = jnp.zeros((2, BLOCK, KV_WIDTH), BF16)
        vext[:, 0:BLOCK, :] = jnp.zeros((2, BLOCK, KV_WIDTH), BF16)
        ubuf[0:POOL_HALO, :] = jnp.zeros((POOL_HALO, POOL_WIDTH), F32)

    x = x_ref[...]
    h = _rms(x, g_ref[...]).astype(BF16)
    proj = jnp.dot(h, win_ref[...], preferred_element_type=F32)
    cos = cos_ref[...]
    sin = sin_ref[...]

    k = _rope(proj[:, ATTN_WIDTH:ATTN_WIDTH + KV_WIDTH], cos, sin)
    v = proj[:, ATTN_WIDTH + KV_WIDTH:ATTN_WIDTH + 2 * KV_WIDTH]
    kext[0, BLOCK:BLOCK + tm, :] = k.astype(BF16)
    kext[1, BLOCK:BLOCK + tm, :] = pltpu.roll(k, HEAD_DIM, 1).astype(BF16)
    vext[0, BLOCK:BLOCK + tm, :] = v.astype(BF16)
    vext[1, BLOCK:BLOCK + tm, :] = pltpu.roll(v, HEAD_DIM, 1).astype(BF16)
    ubuf[POOL_HALO:POOL_HALO + tm, :] = proj[:, ATTN_WIDTH + 2 * KV_WIDTH:]

    lane = lax.broadcasted_iota(jnp.int32, (BLOCK, LANES), 1)
    low = lane < HEAD_DIM
    qi = lax.broadcasted_iota(jnp.int32, (BLOCK, 2 * BLOCK), 0)
    kj = lax.broadcasted_iota(jnp.int32, (BLOCK, 2 * BLOCK), 1)
    band = (kj > qi) & (kj <= qi + BLOCK)
    band_first = band & (kj >= jnp.where(j == 0, BLOCK, 0))
    scale = HEAD_DIM ** -0.5

    for p in range(ATTN_WIDTH // LANES):
        kvh = (2 * p) // (N_Q_HEADS // N_KV_HEADS)
        qp = _rope(proj[:, p * LANES:(p + 1) * LANES], cos, sin) * scale
        for b in range(nblk):
            rows = slice(b * BLOCK, (b + 1) * BLOCK)
            krows = slice(b * BLOCK, b * BLOCK + 2 * BLOCK)
            qb = qp[rows, :]
            mask = band_first if b == 0 else band
            halves = []
            for e in range(2):
                head = 2 * p + e
                qm = jnp.where(low if e == 0 else ~low, qb, 0.0).astype(BF16)
                var = 0 if e == kvh else 1
                s = lax.dot_general(qm, kext[var, krows, :], (((1,), (1,)), ((), ())),
                                    preferred_element_type=F32)
                s = jnp.where(mask, s, NEG)
                sink = sinks_ref[head]
                m = jnp.maximum(jnp.max(s, axis=-1, keepdims=True), sink)
                ex = jnp.exp(s - m)
                den = jnp.sum(ex, axis=-1, keepdims=True) + jnp.exp(sink - m)
                pr = (ex / den).astype(BF16)
                halves.append(jnp.dot(pr, vext[var, krows, :], preferred_element_type=F32))
            oa_ref[rows, p * LANES:(p + 1) * LANES] = jnp.where(low, halves[0], halves[1])

    row = lax.broadcasted_iota(jnp.int32, (tm, POOL_GROUP_DIM), 0) + j * tm
    for gi, w in enumerate(POOL_WINDOWS):
        cols = slice(gi * POOL_GROUP_DIM, (gi + 1) * POOL_GROUP_DIM)
        ug = ubuf[POOL_HALO:POOL_HALO + tm, cols]
        acc = ug
        for i in range(1, w):
            acc = acc + ubuf[POOL_HALO - i:POOL_HALO - i + tm, cols]
        cnt = jnp.minimum(row + 1, w).astype(F32)
        pooled = (acc / cnt - ug).astype(BF16)
        mixed = jnp.dot(pooled, pw_ref[gi], preferred_element_type=F32)
        ob_ref[:, cols] = mixed * ps_ref[:, cols]

    kext[:, 0:BLOCK, :] = kext[:, tm:tm + BLOCK, :]
    vext[:, 0:BLOCK, :] = vext[:, tm:tm + BLOCK, :]
    ubuf[0:POOL_HALO, :] = ubuf[tm:tm + POOL_HALO, :]

    na = _rms(oa_ref[...], an_ref[...]).astype(BF16)
    nb = _rms(ob_ref[...], pn_ref[...]).astype(BF16)
    y = jnp.dot(na, wout_ref[0:ATTN_WIDTH, :], preferred_element_type=F32)
    y = y + jnp.dot(nb, wout_ref[ATTN_WIDTH:, :], preferred_element_type=F32)
    o_ref[...] = x + y


def _mixer(x, cos_t, sin_t, batch, seq, sinks, g, win, pw, ps, an, pn, wout):
    n, d = x.shape
    tm = ROW_TILE
    spt = seq // tm
    row_map = lambda b, j: (b * spt + j, 0)
    return pl.pallas_call(
        _mixer_kernel,
        grid=(batch, spt),
        in_specs=[pl.BlockSpec(memory_space=pltpu.SMEM),
                  pl.BlockSpec((tm, d), row_map),
                  pl.BlockSpec((tm, LANES), row_map),
                  pl.BlockSpec((tm, LANES), row_map),
                  _resident((1, d)), _resident(win.shape), _resident(pw.shape),
                  _resident((1, POOL_WIDTH)), _resident((1, ATTN_WIDTH)),
                  _resident((1, POOL_WIDTH)), _resident(wout.shape)],
        out_specs=pl.BlockSpec((tm, d), row_map),
        out_shape=jax.ShapeDtypeStruct((n, d), F32),
        scratch_shapes=[pltpu.VMEM((2, BLOCK + tm, KV_WIDTH), BF16),
                        pltpu.VMEM((2, BLOCK + tm, KV_WIDTH), BF16),
                        pltpu.VMEM((POOL_HALO + tm, POOL_WIDTH), F32),
                        pltpu.VMEM((tm, ATTN_WIDTH), F32),
                        pltpu.VMEM((tm, POOL_WIDTH), F32)],
        compiler_params=pltpu.CompilerParams(
            dimension_semantics=("arbitrary", "arbitrary"), vmem_limit_bytes=VMEM_LIMIT),
        name="mixer",
    )(sinks, x, cos_t, sin_t, g, win, pw, ps, an, pn, wout)


def _mem_kv_kernel(mem_ref, g_ref, wkv_ref, kt_ref, v_ref):
    mn = _rms(mem_ref[...], g_ref[...]).astype(BF16)
    kv = jnp.dot(mn, wkv_ref[...], preferred_element_type=F32)
    kt_ref[...] = kv[:, :D_MODEL].T.astype(BF16)
    v_ref[...] = kv[:, D_MODEL:].astype(BF16)


def _mem_kv(mem, g, wkv):
    b, m, d = mem.shape
    return pl.pallas_call(
        _mem_kv_kernel,
        grid=(b,),
        in_specs=[pl.BlockSpec((None, m, d), lambda i: (i, 0, 0)),
                  _resident((1, d)), _resident(wkv.shape)],
        out_specs=[pl.BlockSpec((None, d, m), lambda i: (i, 0, 0)),
                   pl.BlockSpec((None, m, d), lambda i: (i, 0, 0))],
        out_shape=[jax.ShapeDtypeStruct((b, d, m), BF16),
                   jax.ShapeDtypeStruct((b, m, d), BF16)],
        compiler_params=pltpu.CompilerParams(
            dimension_semantics=("arbitrary",), vmem_limit_bytes=VMEM_LIMIT),
        name="mem_kv",
    )(mem, g, wkv)


def _xattn_kernel(x_ref, g_ref, wq_ref, kt_ref, v_ref, wo_ref, o_ref, att_ref):
    x = x_ref[...]
    h = _rms(x, g_ref[...]).astype(BF16)
    q = jnp.dot(h, wq_ref[...], preferred_element_type=F32) * (X_HEAD_DIM ** -0.5)
    q = q.astype(BF16)
    for hd in range(X_HEADS):
        cols = slice(hd * X_HEAD_DIM, (hd + 1) * X_HEAD_DIM)
        s = jnp.dot(q[:, cols], kt_ref[cols, :], preferred_element_type=F32)
        m = jnp.max(s, axis=-1, keepdims=True)
        ex = jnp.exp(s - m)
        pr = (ex / jnp.sum(ex, axis=-1, keepdims=True)).astype(BF16)
        att_ref[:, cols] = jnp.dot(pr, v_ref[:, cols], preferred_element_type=F32).astype(BF16)
    o_ref[...] = x + jnp.dot(att_ref[...], wo_ref[...], preferred_element_type=F32)


def _xattn(x, batch, seq, g, wq, kt, v, wo):
    n, d = x.shape
    tm = ROW_TILE
    spt = seq // tm
    m = v.shape[1]
    row_map = lambda b, j: (b * spt + j, 0)
    return pl.pallas_call(
        _xattn_kernel,
        grid=(batch, spt),
        in_specs=[pl.BlockSpec((tm, d), row_map),
                  _resident((1, d)), _resident(wq.shape),
                  pl.BlockSpec((None, d, m), lambda b, j: (b, 0, 0)),
                  pl.BlockSpec((None, m, d), lambda b, j: (b, 0, 0)),
                  _resident(wo.shape)],
        out_specs=pl.BlockSpec((tm, d), row_map),
        out_shape=jax.ShapeDtypeStruct((n, d), F32),
        scratch_shapes=[pltpu.VMEM((tm, d), BF16)],
        compiler_params=pltpu.CompilerParams(
            dimension_semantics=("arbitrary", "arbitrary"), vmem_limit_bytes=VMEM_LIMIT),
        name="xattn",
    )(x, g, wq, kt, v, wo)


def kernel(x, mem, positions, ffn1_norm, ffn1_w_gate, ffn1_w_up, ffn1_w_down, mix_norm, w_in,
           attn_sinks, pool_w, pool_scale, attn_out_norm, pool_out_norm, w_out, xattn_norm,
           mem_norm, xattn_wq, xattn_wkv, xattn_wo, ffn2_norm, ffn2_w_gate, ffn2_w_up,
           ffn2_w_down, final_norm):
    batch, seq, d = x.shape
    depth = w_in.shape[0]
    assert seq % ROW_TILE == 0 and ROW_TILE % BLOCK == 0 and d == D_MODEL
    n = batch * seq
    bf = lambda w: w.astype(BF16)
    row = lambda v: v.reshape(1, -1)

    cos_t, sin_t = _rope_tables(positions)
    xf = x.reshape(n, d)
    fg = row(final_norm)
    for l in range(depth):
        xf = _ffn(xf, row(ffn1_norm[l]), bf(ffn1_w_gate[l]), bf(ffn1_w_up[l]),
                  bf(ffn1_w_down[l]), fg, final=False)
        xf = _mixer(xf, cos_t, sin_t, batch, seq, attn_sinks[l], row(mix_norm[l]), bf(w_in[l]),
                    bf(pool_w[l]), row(pool_scale[l]), row(attn_out_norm[l]),
                    row(pool_out_norm[l]), bf(w_out[l]))
        kt, v = _mem_kv(mem, row(mem_norm[l]), bf(xattn_wkv[l]))
        xf = _xattn(xf, batch, seq, row(xattn_norm[l]), bf(xattn_wq[l]), kt, v, bf(xattn_wo[l]))
        xf = _ffn(xf, row(ffn2_norm[l]), bf(ffn2_w_gate[l]), bf(ffn2_w_up[l]),
                  bf(ffn2_w_down[l]), fg, final=(l == depth - 1))
    return xf.reshape(batch, seq, d)
```

```python
import functools

import jax
import jax.numpy as jnp
from jax import lax
from jax.experimental import pallas as pl
from jax.experimental.pallas import tpu as pltpu

F32 = jnp.float32
BF16 = jnp.bfloat16

D_MODEL = 1024
HEAD_DIM = 64
N_Q_HEADS = 8
N_KV_HEADS = 2
ATTN_WIDTH = N_Q_HEADS * HEAD_DIM
KV_WIDTH = N_KV_HEADS * HEAD_DIM
BLOCK = 128
POOL_WINDOWS = (2, 4, 8, 16)
POOL_GROUP_DIM = 128
POOL_WIDTH = 512
POOL_HALO = 16
POOL_TOP = 32
X_HEADS = 4
X_HEAD_DIM = 256
D_FF = 2816
FFN_RES = 0.5
EPS = 1e-6
ROPE_THETA = 10000.0
NEG = -1e30
LANES = 128

ROW_TILE = 512
VMEM_LIMIT = 56 * 1024 * 1024


def _rms(x, g):
    ms = jnp.mean(x * x, axis=-1, keepdims=True)
    return x * lax.rsqrt(ms + EPS) * g


def _resident(shape):
    nd = len(shape)
    return pl.BlockSpec(shape, lambda *_: (0,) * nd, pipeline_mode=pl.Buffered(1))


def _rope_table_kernel(pos_ref, freq_ref, cos_ref, sin_ref):
    ang = pos_ref[...].astype(F32) * freq_ref[...]
    lane = lax.broadcasted_iota(jnp.int32, ang.shape, 1)
    first_half = (lane % HEAD_DIM) < (HEAD_DIM // 2)
    cos_ref[...] = jnp.cos(ang)
    s = jnp.sin(ang)
    sin_ref[...] = jnp.where(first_half, -s, s)


def _rope_tables(positions):
    n = positions.size
    inv_freq = ROPE_THETA ** (-jnp.arange(0, HEAD_DIM, 2, dtype=F32) / HEAD_DIM)
    freq = jnp.tile(inv_freq, LANES // (HEAD_DIM // 2)).reshape(1, LANES)
    tm = 2048
    return pl.pallas_call(
        _rope_table_kernel,
        grid=(n // tm,),
        in_specs=[pl.BlockSpec((tm, 1), lambda i: (i, 0)),
                  pl.BlockSpec((1, LANES), lambda i: (0, 0))],
        out_specs=[pl.BlockSpec((tm, LANES), lambda i: (i, 0))] * 2,
        out_shape=[jax.ShapeDtypeStruct((n, LANES), F32)] * 2,
        name="rope_tables",
    )(positions.reshape(n, 1), freq)


def _ffn_kernel(x_ref, g_ref, wg_ref, wu_ref, wd_ref, fg_ref, o_ref, *, final):
    x = x_ref[...]
    xn = _rms(x, g_ref[...]).astype(BF16)
    gate = jnp.dot(xn, wg_ref[...], preferred_element_type=F32)
    up = jnp.dot(xn, wu_ref[...], preferred_element_type=F32)
    h = (gate / (1.0 + jnp.exp(-gate)) * up).astype(BF16)
    y = jnp.dot(h, wd_ref[...], preferred_element_type=F32)
    out = x + FFN_RES * y
    if final:
        out = _rms(out, fg_ref[...])
    o_ref[...] = out


def _ffn(x, g, wg, wu, wd, fg, final):
    n, d = x.shape
    tm = ROW_TILE
    return pl.pallas_call(
        functools.partial(_ffn_kernel, final=final),
        grid=(n // tm,),
        in_specs=[pl.BlockSpec((tm, d), lambda i: (i, 0)),
                  _resident((1, d)), _resident(wg.shape), _resident(wu.shape),
                  _resident(wd.shape), _resident((1, d))],
        out_specs=pl.BlockSpec((tm, d), lambda i: (i, 0)),
        out_shape=jax.ShapeDtypeStruct((n, d), F32),
        compiler_params=pltpu.CompilerParams(
            dimension_semantics=("arbitrary",), vmem_limit_bytes=VMEM_LIMIT),
        name="ffn_final" if final else "ffn",
    )(x, g, wg, wu, wd, fg)


def _rope(t, cos, sin_signed):
    lane = lax.broadcasted_iota(jnp.int32, t.shape, 1)
    first_half = (lane % HEAD_DIM) < (HEAD_DIM // 2)
    swapped = jnp.where(first_half,
                        pltpu.roll(t, LANES - HEAD_DIM // 2, 1),
                        pltpu.roll(t, HEAD_DIM // 2, 1))
    return t * cos + swapped * sin_signed


def _mixer_kernel(sinks_ref, x_ref, cos_ref, sin_ref, g_ref, win_ref, pw_ref, ps_ref,
                  an_ref, pn_ref, wout_ref, o_ref,
                  kd, vd, ub, l1, l2, l3, oa_ref, ob_ref):
    j = pl.program_id(1)
    tm = x_ref.shape[0]
    nblk = tm // BLOCK
    group = N_Q_HEADS // N_KV_HEADS

    @pl.when(j == 0)
    def _():
        kd[:, 0:BLOCK, :] = jnp.zeros((N_KV_HEADS, BLOCK, LANES), BF16)
        vd[:, 0:BLOCK, :] = jnp.zeros((N_KV_HEADS, BLOCK, LANES), BF16)
        ub[0:POOL_TOP, :] = jnp.zeros((POOL_TOP, POOL_WIDTH), F32)

    x = x_ref[...]
    h = _rms(x, g_ref[...]).astype(BF16)
    proj = jnp.dot(h, win_ref[...], preferred_element_type=F32)
    cos = cos_ref[...]
    sin = sin_ref[...]

    low_t = lax.broadcasted_iota(jnp.int32, (tm, LANES), 1) < HEAD_DIM
    k = _rope(proj[:, ATTN_WIDTH:ATTN_WIDTH + KV_WIDTH], cos, sin)
    v = proj[:, ATTN_WIDTH + KV_WIDTH:ATTN_WIDTH + 2 * KV_WIDTH]
    kr = pltpu.roll(k, HEAD_DIM, 1)
    vr = pltpu.roll(v, HEAD_DIM, 1)
    kd[0, BLOCK:BLOCK + tm, :] = jnp.where(low_t, k, kr).astype(BF16)
    kd[1, BLOCK:BLOCK + tm, :] = jnp.where(low_t, kr, k).astype(BF16)
    vd[0, BLOCK:BLOCK + tm, :] = jnp.where(low_t, v, vr).astype(BF16)
    vd[1, BLOCK:BLOCK + tm, :] = jnp.where(low_t, vr, v).astype(BF16)
    ub[POOL_TOP:POOL_TOP + tm, :] = proj[:, ATTN_WIDTH + 2 * KV_WIDTH:]

    low = lax.broadcasted_iota(jnp.int32, (BLOCK, LANES), 1) < HEAD_DIM
    tri = (lax.broadcasted_iota(jnp.int32, (BLOCK, BLOCK), 1)
           <= lax.broadcasted_iota(jnp.int32, (BLOCK, BLOCK), 0))
    tri_f = tri.astype(F32)
    first = j == 0
    scale = HEAD_DIM ** -0.5
    zero_bf = jnp.zeros((BLOCK, LANES), BF16)

    q_pairs = [(_rope(proj[:, p * LANES:(p + 1) * LANES], cos, sin) * scale).astype(BF16)
               for p in range(ATTN_WIDTH // LANES)]
    for kvh in range(N_KV_HEADS):
        for b in range(nblk):
            rows = slice(b * BLOCK, (b + 1) * BLOCK)
            krows = slice(b * BLOCK, b * BLOCK + 2 * BLOCK)
            pieces = []
            for idx in range(group):
                qb = q_pairs[(kvh * group + idx) // 2][rows, :]
                pieces.append(jnp.where(low if idx % 2 == 0 else ~low, qb, zero_bf))
            q4 = jnp.concatenate(pieces, axis=0)
            s4 = lax.dot_general(q4, kd[kvh, krows, :], (((1,), (1,)), ((), ())),
                                 preferred_element_type=F32)
            probs = []
            for idx in range(group):
                hrows = slice(idx * BLOCK, (idx + 1) * BLOCK)
                s_prev = s4[hrows, 0:BLOCK]
                if b == 0:
                    s_prev = jnp.where(first, NEG, s_prev)
                s = jnp.where(tri, s4[hrows, BLOCK:2 * BLOCK], s_prev)
                sink = sinks_ref[kvh * group + idx]
                m = jnp.maximum(jnp.max(s, axis=-1, keepdims=True), sink)
                ex = jnp.exp(s - m)
                den = jnp.sum(ex, axis=-1, keepdims=True) + jnp.exp(sink - m)
                pr = ex * (1.0 / den)
                p_cur = pr * tri_f
                probs.append(jnp.concatenate([pr - p_cur, p_cur], axis=1).astype(BF16))
            p4 = jnp.concatenate(probs, axis=0)
            o4 = jnp.dot(p4, vd[kvh, krows, :], preferred_element_type=F32)
            for pp in range(group // 2):
                pcol = (kvh * (group // 2) + pp) * LANES
                oa_ref[rows, pcol:pcol + LANES] = jnp.where(
                    low, o4[(2 * pp) * BLOCK:(2 * pp + 1) * BLOCK, :],
                    o4[(2 * pp + 1) * BLOCK:(2 * pp + 2) * BLOCK, :])

    n1 = tm + POOL_TOP - 8
    l1[8:8 + n1, :] = ub[8:8 + n1, :] + ub[7:7 + n1, :]
    n2 = tm + POOL_TOP - 16
    c2 = POOL_GROUP_DIM
    l2[16:16 + n2, :] = l1[16:16 + n2, c2:] + l1[14:14 + n2, c2:]
    n3 = tm + POOL_TOP - 24
    l3[24:24 + n3, :] = l2[24:24 + n3, c2:] + l2[20:20 + n3, c2:]
    top = POOL_TOP
    sums = [l1[top:top + tm, 0:c2], l2[top:top + tm, 0:c2], l3[top:top + tm, 0:c2],
            l3[top:top + tm, c2:] + l3[top - 8:top - 8 + tm, c2:]]
    row = lax.broadcasted_iota(jnp.int32, (tm, POOL_GROUP_DIM), 0) + j * tm
    for gi, w in enumerate(POOL_WINDOWS):
        cols = slice(gi * POOL_GROUP_DIM, (gi + 1) * POOL_GROUP_DIM)
        cnt = jnp.minimum(row + 1, w).astype(F32)
        pooled = (sums[gi] / cnt - ub[top:top + tm, cols]).astype(BF16)
        mixed = jnp.dot(pooled, pw_ref[gi], preferred_element_type=F32)
        ob_ref[:, cols] = mixed * ps_ref[:, cols]

    kd[:, 0:BLOCK, :] = kd[:, tm:tm + BLOCK, :]
    vd[:, 0:BLOCK, :] = vd[:, tm:tm + BLOCK, :]
    ub[POOL_TOP - POOL_HALO:POOL_TOP, :] = ub[tm + POOL_TOP - POOL_HALO:tm + POOL_TOP, :]

    na = _rms(oa_ref[...], an_ref[...]).astype(BF16)
    nb = _rms(ob_ref[...], pn_ref[...]).astype(BF16)
    y = jnp.dot(na, wout_ref[0:ATTN_WIDTH, :], preferred_element_type=F32)
    y = y + jnp.dot(nb, wout_ref[ATTN_WIDTH:, :], preferred_element_type=F32)
    o_ref[...] = x + y


def _mixer(x, cos_t, sin_t, batch, seq, sinks, g, win, pw, ps, an, pn, wout):
    n, d = x.shape
    tm = ROW_TILE
    spt = seq // tm
    row_map = lambda b, j: (b * spt + j, 0)
    return pl.pallas_call(
        _mixer_kernel,
        grid=(batch, spt),
        in_specs=[pl.BlockSpec(memory_space=pltpu.SMEM),
                  pl.BlockSpec((tm, d), row_map),
                  pl.BlockSpec((tm, LANES), row_map),
                  pl.BlockSpec((tm, LANES), row_map),
                  _resident((1, d)), _resident(win.shape), _resident(pw.shape),
                  _resident((1, POOL_WIDTH)), _resident((1, ATTN_WIDTH)),
                  _resident((1, POOL_WIDTH)), _resident(wout.shape)],
        out_specs=pl.BlockSpec((tm, d), row_map),
        out_shape=jax.ShapeDtypeStruct((n, d), F32),
        scratch_shapes=[pltpu.VMEM((N_KV_HEADS, BLOCK + tm, LANES), BF16),
                        pltpu.VMEM((N_KV_HEADS, BLOCK + tm, LANES), BF16),
                        pltpu.VMEM((POOL_TOP + tm, POOL_WIDTH), F32),
                        pltpu.VMEM((POOL_TOP + tm, POOL_WIDTH), F32),
                        pltpu.VMEM((POOL_TOP + tm, POOL_WIDTH - POOL_GROUP_DIM), F32),
                        pltpu.VMEM((POOL_TOP + tm, POOL_WIDTH - 2 * POOL_GROUP_DIM), F32),
                        pltpu.VMEM((tm, ATTN_WIDTH), F32),
                        pltpu.VMEM((tm, POOL_WIDTH), F32)],
        compiler_params=pltpu.CompilerParams(
            dimension_semantics=("arbitrary", "arbitrary"), vmem_limit_bytes=VMEM_LIMIT),
        name="mixer",
    )(sinks, x, cos_t, sin_t, g, win, pw, ps, an, pn, wout)


def _mem_kv_kernel(mem_ref, g_ref, wkv_ref, kt_ref, v_ref):
    mn = _rms(mem_ref[...], g_ref[...]).astype(BF16)
    kv = jnp.dot(mn, wkv_ref[...], preferred_element_type=F32)
    kt_ref[...] = kv[:, :D_MODEL].T.astype(BF16)
    v_ref[...] = kv[:, D_MODEL:].astype(BF16)


def _mem_kv(mem, g, wkv):
    b, m, d = mem.shape
    return pl.pallas_call(
        _mem_kv_kernel,
        grid=(b,),
        in_specs=[pl.BlockSpec((None, m, d), lambda i: (i, 0, 0)),
                  _resident((1, d)), _resident(wkv.shape)],
        out_specs=[pl.BlockSpec((None, d, m), lambda i: (i, 0, 0)),
                   pl.BlockSpec((None, m, d), lambda i: (i, 0, 0))],
        out_shape=[jax.ShapeDtypeStruct((b, d, m), BF16),
                   jax.ShapeDtypeStruct((b, m, d), BF16)],
        compiler_params=pltpu.CompilerParams(
            dimension_semantics=("arbitrary",), vmem_limit_bytes=VMEM_LIMIT),
        name="mem_kv",
    )(mem, g, wkv)


def _xattn_kernel(x_ref, g_ref, wq_ref, kt_ref, v_ref, wo_ref, o_ref, att_ref):
    x = x_ref[...]
    h = _rms(x, g_ref[...]).astype(BF16)
    q = jnp.dot(h, wq_ref[...], preferred_element_type=F32) * (X_HEAD_DIM ** -0.5)
    q = q.astype(BF16)
    for hd in range(X_HEADS):
        cols = slice(hd * X_HEAD_DIM, (hd + 1) * X_HEAD_DIM)
        s = jnp.dot(q[:, cols], kt_ref[cols, :], preferred_element_type=F32)
        m = jnp.max(s, axis=-1, keepdims=True)
        ex = jnp.exp(s - m)
        pr = (ex / jnp.sum(ex, axis=-1, keepdims=True)).astype(BF16)
        att_ref[:, cols] = jnp.dot(pr, v_ref[:, cols], preferred_element_type=F32).astype(BF16)
    o_ref[...] = x + jnp.dot(att_ref[...], wo_ref[...], preferred_element_type=F32)


def _xattn(x, batch, seq, g, wq, kt, v, wo):
    n, d = x.shape
    tm = ROW_TILE
    spt = seq // tm
    m = v.shape[1]
    row_map = lambda b, j: (b * spt + j, 0)
    return pl.pallas_call(
        _xattn_kernel,
        grid=(batch, spt),
        in_specs=[pl.BlockSpec((tm, d), row_map),
                  _resident((1, d)), _resident(wq.shape),
                  pl.BlockSpec((None, d, m), lambda b, j: (b, 0, 0)),
                  pl.BlockSpec((None, m, d), lambda b, j: (b, 0, 0)),
                  _resident(wo.shape)],
        out_specs=pl.BlockSpec((tm, d), row_map),
        out_shape=jax.ShapeDtypeStruct((n, d), F32),
        scratch_shapes=[pltpu.VMEM((tm, d), BF16)],
        compiler_params=pltpu.CompilerParams(
            dimension_semantics=("arbitrary", "arbitrary"), vmem_limit_bytes=VMEM_LIMIT),
        name="xattn",
    )(x, g, wq, kt, v, wo)


def kernel(x, mem, positions, ffn1_norm, ffn1_w_gate, ffn1_w_up, ffn1_w_down, mix_norm, w_in,
           attn_sinks, pool_w, pool_scale, attn_out_norm, pool_out_norm, w_out, xattn_norm,
           mem_norm, xattn_wq, xattn_wkv, xattn_wo, ffn2_norm, ffn2_w_gate, ffn2_w_up,
           ffn2_w_down, final_norm):
    batch, seq, d = x.shape
    depth = w_in.shape[0]
    assert seq % ROW_TILE == 0 and ROW_TILE % BLOCK == 0 and d == D_MODEL
    n = batch * seq
    bf = lambda w: w.astype(BF16)
    row = lambda v: v.reshape(1, -1)

    cos_t, sin_t = _rope_tables(positions)
    xf = x.reshape(n, d)
    fg = row(final_norm)
    for l in range(depth):
        xf = _ffn(xf, row(ffn1_norm[l]), bf(ffn1_w_gate[l]), bf(ffn1_w_up[l]),
                  bf(ffn1_w_down[l]), fg, final=False)
        xf = _mixer(xf, cos_t, sin_t, batch, seq, attn_sinks[l], row(mix_norm[l]), bf(w_in[l]),
                    bf(pool_w[l]), row(pool_scale[l]), row(attn_out_norm[l]),
                    row(pool_out_norm[l]), bf(w_out[l]))
        kt, v = _mem_kv(mem, row(mem_norm[l]), bf(xattn_wkv[l]))
        xf = _xattn(xf, batch, seq, row(xattn_norm[l]), bf(xattn_wq[l]), kt, v, bf(xattn_wo[l]))
        xf = _ffn(xf, row(ffn2_norm[l]), bf(ffn2_w_gate[l]), bf(ffn2_w_up[l]),
                  bf(ffn2_w_down[l]), fg, final=(l == depth - 1))
    return xf.reshape(batch, seq, d)
```

```python
import functools

import jax
import jax.numpy as jnp
from jax import lax
from jax.experimental import pallas as pl
from jax.experimental.pallas import tpu as pltpu

F32 = jnp.float32
BF16 = jnp.bfloat16

D_MODEL = 1024
HEAD_DIM = 64
N_Q_HEADS = 8
N_KV_HEADS = 2
ATTN_WIDTH = N_Q_HEADS * HEAD_DIM
KV_WIDTH = N_KV_HEADS * HEAD_DIM
BLOCK = 128
POOL_WINDOWS = (2, 4, 8, 16)
POOL_GROUP_DIM = 128
POOL_WIDTH = 512
POOL_HALO = 16
POOL_TOP = 32
X_HEADS = 4
X_HEAD_DIM = 256
D_FF = 2816
FFN_RES = 0.5
EPS = 1e-6
ROPE_THETA = 10000.0
NEG = -1e30
LANES = 128

ROW_TILE = 512
VMEM_LIMIT = 56 * 1024 * 1024


def _rms(x, g):
    ms = jnp.mean(x * x, axis=-1, keepdims=True)
    return x * lax.rsqrt(ms + EPS) * g


def _resident(shape):
    nd = len(shape)
    return pl.BlockSpec(shape, lambda *_: (0,) * nd, pipeline_mode=pl.Buffered(1))


def _rope_table_kernel(pos_ref, freq_ref, cos_ref, sin_ref):
    ang = pos_ref[...].astype(F32) * freq_ref[...]
    lane = lax.broadcasted_iota(jnp.int32, ang.shape, 1)
    first_half = (lane % HEAD_DIM) < (HEAD_DIM // 2)
    cos_ref[...] = jnp.cos(ang)
    s = jnp.sin(ang)
    sin_ref[...] = jnp.where(first_half, -s, s)


def _rope_tables(positions):
    n = positions.size
    inv_freq = ROPE_THETA ** (-jnp.arange(0, HEAD_DIM, 2, dtype=F32) / HEAD_DIM)
    freq = jnp.tile(inv_freq, LANES // (HEAD_DIM // 2)).reshape(1, LANES)
    tm = 2048
    return pl.pallas_call(
        _rope_table_kernel,
        grid=(n // tm,),
        in_specs=[pl.BlockSpec((tm, 1), lambda i: (i, 0)),
                  pl.BlockSpec((1, LANES), lambda i: (0, 0))],
        out_specs=[pl.BlockSpec((tm, LANES), lambda i: (i, 0))] * 2,
        out_shape=[jax.ShapeDtypeStruct((n, LANES), F32)] * 2,
        name="rope_tables",
    )(positions.reshape(n, 1), freq)


def _ffn_kernel(x_ref, g_ref, wg_ref, wu_ref, wd_ref, fg_ref, o_ref, *, final):
    x = x_ref[...]
    xn = _rms(x, g_ref[...]).astype(BF16)
    gate = jnp.dot(xn, wg_ref[...], preferred_element_type=F32)
    up = jnp.dot(xn, wu_ref[...], preferred_element_type=F32)
    h = (gate / (1.0 + jnp.exp(-gate)) * up).astype(BF16)
    y = jnp.dot(h, wd_ref[...], preferred_element_type=F32)
    out = x + FFN_RES * y
    if final:
        out = _rms(out, fg_ref[...])
    o_ref[...] = out


def _ffn(x, g, wg, wu, wd, fg, final):
    n, d = x.shape
    tm = ROW_TILE
    return pl.pallas_call(
        functools.partial(_ffn_kernel, final=final),
        grid=(n // tm,),
        in_specs=[pl.BlockSpec((tm, d), lambda i: (i, 0)),
                  _resident((1, d)), _resident(wg.shape), _resident(wu.shape),
                  _resident(wd.shape), _resident((1, d))],
        out_specs=pl.BlockSpec((tm, d), lambda i: (i, 0)),
        out_shape=jax.ShapeDtypeStruct((n, d), F32),
        compiler_params=pltpu.CompilerParams(
            dimension_semantics=("arbitrary",), vmem_limit_bytes=VMEM_LIMIT),
        name="ffn_final" if final else "ffn",
    )(x, g, wg, wu, wd, fg)


def _rope(t, cos, sin_signed):
    lane = lax.broadcasted_iota(jnp.int32, t.shape, 1)
    first_half = (lane % HEAD_DIM) < (HEAD_DIM // 2)
    swapped = jnp.where(first_half,
                        pltpu.roll(t, LANES - HEAD_DIM // 2, 1),
                        pltpu.roll(t, HEAD_DIM // 2, 1))
    return t * cos + swapped * sin_signed


def _mixer_kernel(sinks_ref, x_ref, cos_ref, sin_ref, g_ref, win_ref, pw_ref, ps_ref,
                  an_ref, pn_ref, wout_ref, o_ref,
                  kd, vd, ub, l1, l2, l3, oa_ref, ob_ref):
    j = pl.program_id(1)
    tm = x_ref.shape[0]
    nblk = tm // BLOCK
    group = N_Q_HEADS // N_KV_HEADS

    @pl.when(j == 0)
    def _():
        kd[:, 0:BLOCK, :] = jnp.zeros((N_KV_HEADS, BLOCK, LANES), BF16)
        vd[:, 0:BLOCK, :] = jnp.zeros((N_KV_HEADS, BLOCK, LANES), BF16)
        ub[0:POOL_TOP, :] = jnp.zeros((POOL_TOP, POOL_WIDTH), F32)

    x = x_ref[...]
    h = _rms(x, g_ref[...]).astype(BF16)
    proj = jnp.dot(h, win_ref[...], preferred_element_type=F32)
    cos = cos_ref[...]
    sin = sin_ref[...]

    low_t = lax.broadcasted_iota(jnp.int32, (tm, LANES), 1) < HEAD_DIM
    k = _rope(proj[:, ATTN_WIDTH:ATTN_WIDTH + KV_WIDTH], cos, sin)
    v = proj[:, ATTN_WIDTH + KV_WIDTH:ATTN_WIDTH + 2 * KV_WIDTH]
    kr = pltpu.roll(k, HEAD_DIM, 1)
    vr = pltpu.roll(v, HEAD_DIM, 1)
    kd[0, BLOCK:BLOCK + tm, :] = jnp.where(low_t, k, kr).astype(BF16)
    kd[1, BLOCK:BLOCK + tm, :] = jnp.where(low_t, kr, k).astype(BF16)
    vd[0, BLOCK:BLOCK + tm, :] = jnp.where(low_t, v, vr).astype(BF16)
    vd[1, BLOCK:BLOCK + tm, :] = jnp.where(low_t, vr, v).astype(BF16)
    ub[POOL_TOP:POOL_TOP + tm, :] = proj[:, ATTN_WIDTH + 2 * KV_WIDTH:]

    low = lax.broadcasted_iota(jnp.int32, (BLOCK, LANES), 1) < HEAD_DIM
    tri = (lax.broadcasted_iota(jnp.int32, (BLOCK, BLOCK), 1)
           <= lax.broadcasted_iota(jnp.int32, (BLOCK, BLOCK), 0))
    tri_f = tri.astype(F32)
    tri_t = (lax.broadcasted_iota(jnp.int32, (BLOCK, BLOCK), 0)
             <= lax.broadcasted_iota(jnp.int32, (BLOCK, BLOCK), 1))
    tri_tf = tri_t.astype(F32)
    first = j == 0
    scale = HEAD_DIM ** -0.5
    zero_bf = jnp.zeros((BLOCK, LANES), BF16)

    q_pairs = [(_rope(proj[:, p * LANES:(p + 1) * LANES], cos, sin) * scale).astype(BF16)
               for p in range(ATTN_WIDTH // LANES)]
    for kvh in range(N_KV_HEADS):
        for b in range(nblk):
            rows = slice(b * BLOCK, (b + 1) * BLOCK)
            krows = slice(b * BLOCK, b * BLOCK + 2 * BLOCK)
            pieces = []
            for idx in range(group):
                qb = q_pairs[(kvh * group + idx) // 2][rows, :]
                pieces.append(jnp.where(low if idx % 2 == 0 else ~low, qb, zero_bf))
            q4 = jnp.concatenate(pieces, axis=0)
            s4 = lax.dot_general(kd[kvh, krows, :], q4, (((1,), (1,)), ((), ())),
                                 preferred_element_type=F32)
            probs = []
            for idx in range(group):
                hcols = slice(idx * BLOCK, (idx + 1) * BLOCK)
                s_prev = s4[0:BLOCK, hcols]
                if b == 0:
                    s_prev = jnp.where(first, NEG, s_prev)
                s = jnp.where(tri_t, s4[BLOCK:2 * BLOCK, hcols], s_prev)
                sink = sinks_ref[kvh * group + idx]
                m = jnp.maximum(jnp.max(s, axis=0, keepdims=True), sink)
                ex = jnp.exp(s - m)
                den = jnp.sum(ex, axis=0, keepdims=True) + jnp.exp(sink - m)
                pr = ex * (1.0 / den)
                p_cur = pr * tri_tf
                probs.append(jnp.concatenate([pr - p_cur, p_cur], axis=0).astype(BF16))
            p4 = jnp.concatenate(probs, axis=1)
            o4 = lax.dot_general(p4, vd[kvh, krows, :], (((0,), (0,)), ((), ())),
                                 preferred_element_type=F32)
            for pp in range(group // 2):
                pcol = (kvh * (group // 2) + pp) * LANES
                oa_ref[rows, pcol:pcol + LANES] = jnp.where(
                    low, o4[(2 * pp) * BLOCK:(2 * pp + 1) * BLOCK, :],
                    o4[(2 * pp + 1) * BLOCK:(2 * pp + 2) * BLOCK, :])

    n1 = tm + POOL_TOP - 8
    l1[8:8 + n1, :] = ub[8:8 + n1, :] + ub[7:7 + n1, :]
    n2 = tm + POOL_TOP - 16
    c2 = POOL_GROUP_DIM
    l2[16:16 + n2, :] = l1[16:16 + n2, c2:] + l1[14:14 + n2, c2:]
    n3 = tm + POOL_TOP - 24
    l3[24:24 + n3, :] = l2[24:24 + n3, c2:] + l2[20:20 + n3, c2:]
    top = POOL_TOP
    sums = [l1[top:top + tm, 0:c2], l2[top:top + tm, 0:c2], l3[top:top + tm, 0:c2],
            l3[top:top + tm, c2:] + l3[top - 8:top - 8 + tm, c2:]]
    row = lax.broadcasted_iota(jnp.int32, (tm, POOL_GROUP_DIM), 0) + j * tm
    for gi, w in enumerate(POOL_WINDOWS):
        cols = slice(gi * POOL_GROUP_DIM, (gi + 1) * POOL_GROUP_DIM)
        cnt = jnp.minimum(row + 1, w).astype(F32)
        pooled = (sums[gi] / cnt - ub[top:top + tm, cols]).astype(BF16)
        mixed = jnp.dot(pooled, pw_ref[gi], preferred_element_type=F32)
        ob_ref[:, cols] = mixed * ps_ref[:, cols]

    kd[:, 0:BLOCK, :] = kd[:, tm:tm + BLOCK, :]
    vd[:, 0:BLOCK, :] = vd[:, tm:tm + BLOCK, :]
    ub[POOL_TOP - POOL_HALO:POOL_TOP, :] = ub[tm + POOL_TOP - POOL_HALO:tm + POOL_TOP, :]

    na = _rms(oa_ref[...], an_ref[...]).astype(BF16)
    nb = _rms(ob_ref[...], pn_ref[...]).astype(BF16)
    y = jnp.dot(na, wout_ref[0:ATTN_WIDTH, :], preferred_element_type=F32)
    y = y + jnp.dot(nb, wout_ref[ATTN_WIDTH:, :], preferred_element_type=F32)
    o_ref[...] = x + y


def _mixer(x, cos_t, sin_t, batch, seq, sinks, g, win, pw, ps, an, pn, wout):
    n, d = x.shape
    tm = ROW_TILE
    spt = seq // tm
    row_map = lambda b, j: (b * spt + j, 0)
    return pl.pallas_call(
        _mixer_kernel,
        grid=(batch, spt),
        in_specs=[pl.BlockSpec(memory_space=pltpu.SMEM),
                  pl.BlockSpec((tm, d), row_map),
                  pl.BlockSpec((tm, LANES), row_map),
                  pl.BlockSpec((tm, LANES), row_map),
                  _resident((1, d)), _resident(win.shape), _resident(pw.shape),
                  _resident((1, POOL_WIDTH)), _resident((1, ATTN_WIDTH)),
                  _resident((1, POOL_WIDTH)), _resident(wout.shape)],
        out_specs=pl.BlockSpec((tm, d), row_map),
        out_shape=jax.ShapeDtypeStruct((n, d), F32),
        scratch_shapes=[pltpu.VMEM((N_KV_HEADS, BLOCK + tm, LANES), BF16),
                        pltpu.VMEM((N_KV_HEADS, BLOCK + tm, LANES), BF16),
                        pltpu.VMEM((POOL_TOP + tm, POOL_WIDTH), F32),
                        pltpu.VMEM((POOL_TOP + tm, POOL_WIDTH), F32),
                        pltpu.VMEM((POOL_TOP + tm, POOL_WIDTH - POOL_GROUP_DIM), F32),
                        pltpu.VMEM((POOL_TOP + tm, POOL_WIDTH - 2 * POOL_GROUP_DIM), F32),
                        pltpu.VMEM((tm, ATTN_WIDTH), F32),
                        pltpu.VMEM((tm, POOL_WIDTH), F32)],
        compiler_params=pltpu.CompilerParams(
            dimension_semantics=("arbitrary", "arbitrary"), vmem_limit_bytes=VMEM_LIMIT),
        name="mixer",
    )(sinks, x, cos_t, sin_t, g, win, pw, ps, an, pn, wout)


def _mem_kv_kernel(mem_ref, g_ref, wkv_ref, kv_ref):
    mn = _rms(mem_ref[...], g_ref[...]).astype(BF16)
    kv_ref[...] = jnp.dot(mn, wkv_ref[...], preferred_element_type=F32).astype(BF16)


def _mem_kv(mem, g, wkv):
    b, m, d = mem.shape
    return pl.pallas_call(
        _mem_kv_kernel,
        grid=(b,),
        in_specs=[pl.BlockSpec((None, m, d), lambda i: (i, 0, 0)),
                  _resident((1, d)), _resident(wkv.shape)],
        out_specs=pl.BlockSpec((None, m, 2 * d), lambda i: (i, 0, 0)),
        out_shape=jax.ShapeDtypeStruct((b, m, 2 * d), BF16),
        compiler_params=pltpu.CompilerParams(
            dimension_semantics=("arbitrary",), vmem_limit_bytes=VMEM_LIMIT),
        name="mem_kv",
    )(mem, g, wkv)


def _xattn_kernel(x_ref, g_ref, wq_ref, k_ref, v_ref, wo_ref, o_ref, att_ref):
    x = x_ref[...]
    h = _rms(x, g_ref[...]).astype(BF16)
    q = jnp.dot(h, wq_ref[...], preferred_element_type=F32) * (X_HEAD_DIM ** -0.5)
    q = q.astype(BF16)
    for hd in range(X_HEADS):
        cols = slice(hd * X_HEAD_DIM, (hd + 1) * X_HEAD_DIM)
        s = lax.dot_general(k_ref[:, cols], q[:, cols], (((1,), (1,)), ((), ())),
                            preferred_element_type=F32)
        m = jnp.max(s, axis=0, keepdims=True)
        ex = jnp.exp(s - m)
        pr = (ex * (1.0 / jnp.sum(ex, axis=0, keepdims=True))).astype(BF16)
        att_ref[:, cols] = lax.dot_general(pr, v_ref[:, cols], (((0,), (0,)), ((), ())),
                                           preferred_element_type=F32).astype(BF16)
    o_ref[...] = x + jnp.dot(att_ref[...], wo_ref[...], preferred_element_type=F32)


def _xattn(x, batch, seq, g, wq, kv, wo):
    n, d = x.shape
    tm = ROW_TILE
    spt = seq // tm
    m = kv.shape[1]
    row_map = lambda b, j: (b * spt + j, 0)
    return pl.pallas_call(
        _xattn_kernel,
        grid=(batch, spt),
        in_specs=[pl.BlockSpec((tm, d), row_map),
                  _resident((1, d)), _resident(wq.shape),
                  pl.BlockSpec((None, m, d), lambda b, j: (b, 0, 0)),
                  pl.BlockSpec((None, m, d), lambda b, j: (b, 0, 1)),
                  _resident(wo.shape)],
        out_specs=pl.BlockSpec((tm, d), row_map),
        out_shape=jax.ShapeDtypeStruct((n, d), F32),
        scratch_shapes=[pltpu.VMEM((tm, d), BF16)],
        compiler_params=pltpu.CompilerParams(
            dimension_semantics=("arbitrary", "arbitrary"), vmem_limit_bytes=VMEM_LIMIT),
        name="xattn",
    )(x, g, wq, kv, kv, wo)


def kernel(x, mem, positions, ffn1_norm, ffn1_w_gate, ffn1_w_up, ffn1_w_down, mix_norm, w_in,
           attn_sinks, pool_w, pool_scale, attn_out_norm, pool_out_norm, w_out, xattn_norm,
           mem_norm, xattn_wq, xattn_wkv, xattn_wo, ffn2_norm, ffn2_w_gate, ffn2_w_up,
           ffn2_w_down, final_norm):
    batch, seq, d = x.shape
    depth = w_in.shape[0]
    assert seq % ROW_TILE == 0 and ROW_TILE % BLOCK == 0 and d == D_MODEL
    n = batch * seq
    bf = lambda w: w.astype(BF16)
    row = lambda v: v.reshape(1, -1)

    cos_t, sin_t = _rope_tables(positions)
    xf = x.reshape(n, d)
    fg = row(final_norm)
    for l in range(depth):
        xf = _ffn(xf, row(ffn1_norm[l]), bf(ffn1_w_gate[l]), bf(ffn1_w_up[l]),
                  bf(ffn1_w_down[l]), fg, final=False)
        xf = _mixer(xf, cos_t, sin_t, batch, seq, attn_sinks[l], row(mix_norm[l]), bf(w_in[l]),
                    bf(pool_w[l]), row(pool_scale[l]), row(attn_out_norm[l]),
                    row(pool_out_norm[l]), bf(w_out[l]))
        kv = _mem_kv(mem, row(mem_norm[l]), bf(xattn_wkv[l]))
        xf = _xattn(xf, batch, seq, row(xattn_norm[l]), bf(xattn_wq[l]), kv, bf(xattn_wo[l]))
        xf = _ffn(xf, row(ffn2_norm[l]), bf(ffn2_w_gate[l]), bf(ffn2_w_up[l]),
                  bf(ffn2_w_down[l]), fg, final=(l == depth - 1))
    return xf.reshape(batch, seq, d)
```

```python
import functools

import jax
import jax.numpy as jnp
from jax import lax
from jax.experimental import pallas as pl
from jax.experimental.pallas import tpu as pltpu

F32 = jnp.float32
BF16 = jnp.bfloat16

D_MODEL = 1024
HEAD_DIM = 64
N_Q_HEADS = 8
N_KV_HEADS = 2
ATTN_WIDTH = N_Q_HEADS * HEAD_DIM
KV_WIDTH = N_KV_HEADS * HEAD_DIM
BLOCK = 128
POOL_WINDOWS = (2, 4, 8, 16)
POOL_GROUP_DIM = 128
POOL_WIDTH = 512
POOL_HALO = 16
POOL_TOP = 32
X_HEADS = 4
X_HEAD_DIM = 256
D_FF = 2816
FFN_RES = 0.5
EPS = 1e-6
ROPE_THETA = 10000.0
NEG = -1e30
LANES = 128

ROW_TILE = 1024
MXU_TILE = 256
FF_CHUNKS = tuple((lo, min(lo + 3 * MXU_TILE, D_FF)) for lo in range(0, D_FF, 3 * MXU_TILE))
VMEM_LIMIT = 56 * 1024 * 1024


def _rms(x, g):
    ms = jnp.mean(x * x, axis=-1, keepdims=True)
    return x * lax.rsqrt(ms + EPS) * g


def _resident(shape):
    nd = len(shape)
    return pl.BlockSpec(shape, lambda *_: (0,) * nd, pipeline_mode=pl.Buffered(1))


def _rope_table_kernel(pos_ref, freq_ref, cos_ref, sin_ref):
    ang = pos_ref[...].astype(F32) * freq_ref[...]
    lane = lax.broadcasted_iota(jnp.int32, ang.shape, 1)
    first_half = (lane % HEAD_DIM) < (HEAD_DIM // 2)
    cos_ref[...] = jnp.cos(ang)
    s = jnp.sin(ang)
    sin_ref[...] = jnp.where(first_half, -s, s)


def _rope_tables(positions):
    n = positions.size
    inv_freq = ROPE_THETA ** (-jnp.arange(0, HEAD_DIM, 2, dtype=F32) / HEAD_DIM)
    freq = jnp.tile(inv_freq, LANES // (HEAD_DIM // 2)).reshape(1, LANES)
    tm = 2048
    return pl.pallas_call(
        _rope_table_kernel,
        grid=(n // tm,),
        in_specs=[pl.BlockSpec((tm, 1), lambda i: (i, 0)),
                  pl.BlockSpec((1, LANES), lambda i: (0, 0))],
        out_specs=[pl.BlockSpec((tm, LANES), lambda i: (i, 0))] * 2,
        out_shape=[jax.ShapeDtypeStruct((n, LANES), F32)] * 2,
        name="rope_tables",
    )(positions.reshape(n, 1), freq)


def _ffn_body(x, g, wg_ref, wu_ref, wd_ref):
    xn = _rms(x, g).astype(BF16)
    y = None
    for lo, hi in FF_CHUNKS:
        gate = jnp.dot(xn, wg_ref[:, lo:hi], preferred_element_type=F32)
        up = jnp.dot(xn, wu_ref[:, lo:hi], preferred_element_type=F32)
        h = (gate / (1.0 + jnp.exp(-gate)) * up).astype(BF16)
        part = jnp.dot(h, wd_ref[lo:hi, :], preferred_element_type=F32)
        y = part if y is None else y + part
    return x + FFN_RES * y


def _ffn_kernel(x_ref, g_ref, wg_ref, wu_ref, wd_ref, fg_ref, o_ref, *, final):
    out = _ffn_body(x_ref[...], g_ref[...], wg_ref, wu_ref, wd_ref)
    if final:
        out = _rms(out, fg_ref[...])
    o_ref[...] = out


def _ffn(x, g, wg, wu, wd, fg, final):
    n, d = x.shape
    tm = ROW_TILE
    return pl.pallas_call(
        functools.partial(_ffn_kernel, final=final),
        grid=(n // tm,),
        in_specs=[pl.BlockSpec((tm, d), lambda i: (i, 0)),
                  _resident((1, d)), _resident(wg.shape), _resident(wu.shape),
                  _resident(wd.shape), _resident((1, d))],
        out_specs=pl.BlockSpec((tm, d), lambda i: (i, 0)),
        out_shape=jax.ShapeDtypeStruct((n, d), F32),
        compiler_params=pltpu.CompilerParams(
            dimension_semantics=("arbitrary",), vmem_limit_bytes=VMEM_LIMIT),
        name="ffn_final" if final else "ffn",
    )(x, g, wg, wu, wd, fg)


def _rope(t, cos, sin_signed):
    lane = lax.broadcasted_iota(jnp.int32, t.shape, 1)
    first_half = (lane % HEAD_DIM) < (HEAD_DIM // 2)
    swapped = jnp.where(first_half,
                        pltpu.roll(t, LANES - HEAD_DIM // 2, 1),
                        pltpu.roll(t, HEAD_DIM // 2, 1))
    return t * cos + swapped * sin_signed


def _mixer_kernel(sinks_ref, x_ref, cos_ref, sin_ref, g_ref, win_ref, pw_ref, ps_ref,
                  an_ref, pn_ref, wout_ref, o_ref,
                  kd, vd, ub, l1, l2, l3, oa_ref, ob_ref):
    j = pl.program_id(1)
    tm = x_ref.shape[0]
    nblk = tm // BLOCK
    group = N_Q_HEADS // N_KV_HEADS

    @pl.when(j == 0)
    def _():
        kd[:, 0:BLOCK, :] = jnp.zeros((N_KV_HEADS, BLOCK, LANES), BF16)
        vd[:, 0:BLOCK, :] = jnp.zeros((N_KV_HEADS, BLOCK, LANES), BF16)
        ub[0:POOL_TOP, :] = jnp.zeros((POOL_TOP, POOL_WIDTH), F32)

    x = x_ref[...]
    h = _rms(x, g_ref[...]).astype(BF16)
    proj = jnp.dot(h, win_ref[...], preferred_element_type=F32)
    cos = cos_ref[...]
    sin = sin_ref[...]

    low_t = lax.broadcasted_iota(jnp.int32, (tm, LANES), 1) < HEAD_DIM
    k = _rope(proj[:, ATTN_WIDTH:ATTN_WIDTH + KV_WIDTH], cos, sin)
    v = proj[:, ATTN_WIDTH + KV_WIDTH:ATTN_WIDTH + 2 * KV_WIDTH]
    kr = pltpu.roll(k, HEAD_DIM, 1)
    vr = pltpu.roll(v, HEAD_DIM, 1)
    kd[0, BLOCK:BLOCK + tm, :] = jnp.where(low_t, k, kr).astype(BF16)
    kd[1, BLOCK:BLOCK + tm, :] = jnp.where(low_t, kr, k).astype(BF16)
    vd[0, BLOCK:BLOCK + tm, :] = jnp.where(low_t, v, vr).astype(BF16)
    vd[1, BLOCK:BLOCK + tm, :] = jnp.where(low_t, vr, v).astype(BF16)
    ub[POOL_TOP:POOL_TOP + tm, :] = proj[:, ATTN_WIDTH + 2 * KV_WIDTH:]

    low = lax.broadcasted_iota(jnp.int32, (BLOCK, LANES), 1) < HEAD_DIM
    tri = (lax.broadcasted_iota(jnp.int32, (BLOCK, BLOCK), 1)
           <= lax.broadcasted_iota(jnp.int32, (BLOCK, BLOCK), 0))
    tri_f = tri.astype(F32)
    tri_t = (lax.broadcasted_iota(jnp.int32, (BLOCK, BLOCK), 0)
             <= lax.broadcasted_iota(jnp.int32, (BLOCK, BLOCK), 1))
    tri_tf = tri_t.astype(F32)
    first = j == 0
    scale = HEAD_DIM ** -0.5
    zero_bf = jnp.zeros((BLOCK, LANES), BF16)

    q_pairs = [(_rope(proj[:, p * LANES:(p + 1) * LANES], cos, sin) * scale).astype(BF16)
               for p in range(ATTN_WIDTH // LANES)]
    for kvh in range(N_KV_HEADS):
        for b in range(nblk):
            rows = slice(b * BLOCK, (b + 1) * BLOCK)
            krows = slice(b * BLOCK, b * BLOCK + 2 * BLOCK)
            pieces = []
            for idx in range(group):
                qb = q_pairs[(kvh * group + idx) // 2][rows, :]
                pieces.append(jnp.where(low if idx % 2 == 0 else ~low, qb, zero_bf))
            q4 = jnp.concatenate(pieces, axis=0)
            s4 = lax.dot_general(kd[kvh, krows, :], q4, (((1,), (1,)), ((), ())),
                                 preferred_element_type=F32)
            probs = []
            for idx in range(group):
                hcols = slice(idx * BLOCK, (idx + 1) * BLOCK)
                s_prev = s4[0:BLOCK, hcols]
                if b == 0:
                    s_prev = jnp.where(first, NEG, s_prev)
                s = jnp.where(tri_t, s4[BLOCK:2 * BLOCK, hcols], s_prev)
                sink = sinks_ref[kvh * group + idx]
                m = jnp.maximum(jnp.max(s, axis=0, keepdims=True), sink)
                ex = jnp.exp(s - m)
                den = jnp.sum(ex, axis=0, keepdims=True) + jnp.exp(sink - m)
                pr = ex * (1.0 / den)
                p_cur = pr * tri_tf
                probs.append(jnp.concatenate([pr - p_cur, p_cur], axis=0).astype(BF16))
            p4 = jnp.concatenate(probs, axis=1)
            o4 = lax.dot_general(p4, vd[kvh, krows, :], (((0,), (0,)), ((), ())),
                                 preferred_element_type=F32)
            for pp in range(group // 2):
                pcol = (kvh * (group // 2) + pp) * LANES
                oa_ref[rows, pcol:pcol + LANES] = jnp.where(
                    low, o4[(2 * pp) * BLOCK:(2 * pp + 1) * BLOCK, :],
                    o4[(2 * pp + 1) * BLOCK:(2 * pp + 2) * BLOCK, :])

    n1 = tm + POOL_TOP - 8
    l1[8:8 + n1, :] = ub[8:8 + n1, :] + ub[7:7 + n1, :]
    n2 = tm + POOL_TOP - 16
    c2 = POOL_GROUP_DIM
    l2[16:16 + n2, :] = l1[16:16 + n2, c2:] + l1[14:14 + n2, c2:]
    n3 = tm + POOL_TOP - 24
    l3[24:24 + n3, :] = l2[24:24 + n3, c2:] + l2[20:20 + n3, c2:]
    top = POOL_TOP
    sums = [l1[top:top + tm, 0:c2], l2[top:top + tm, 0:c2], l3[top:top + tm, 0:c2],
            l3[top:top + tm, c2:] + l3[top - 8:top - 8 + tm, c2:]]
    row = lax.broadcasted_iota(jnp.int32, (tm, POOL_GROUP_DIM), 0) + j * tm
    for gi, w in enumerate(POOL_WINDOWS):
        cols = slice(gi * POOL_GROUP_DIM, (gi + 1) * POOL_GROUP_DIM)
        cnt = jnp.minimum(row + 1, w).astype(F32)
        pooled = (sums[gi] / cnt - ub[top:top + tm, cols]).astype(BF16)
        mixed = jnp.dot(pooled, pw_ref[gi], preferred_element_type=F32)
        ob_ref[:, cols] = mixed * ps_ref[:, cols]

    kd[:, 0:BLOCK, :] = kd[:, tm:tm + BLOCK, :]
    vd[:, 0:BLOCK, :] = vd[:, tm:tm + BLOCK, :]
    ub[POOL_TOP - POOL_HALO:POOL_TOP, :] = ub[tm + POOL_TOP - POOL_HALO:tm + POOL_TOP, :]

    na = _rms(oa_ref[...], an_ref[...]).astype(BF16)
    nb = _rms(ob_ref[...], pn_ref[...]).astype(BF16)
    y = jnp.dot(na, wout_ref[0:ATTN_WIDTH, :], preferred_element_type=F32)
    y = y + jnp.dot(nb, wout_ref[ATTN_WIDTH:, :], preferred_element_type=F32)
    o_ref[...] = x + y


def _mixer(x, cos_t, sin_t, batch, seq, sinks, g, win, pw, ps, an, pn, wout):
    n, d = x.shape
    tm = ROW_TILE
    spt = seq // tm
    row_map = lambda b, j: (b * spt + j, 0)
    return pl.pallas_call(
        _mixer_kernel,
        grid=(batch, spt),
        in_specs=[pl.BlockSpec(memory_space=pltpu.SMEM),
                  pl.BlockSpec((tm, d), row_map),
                  pl.BlockSpec((tm, LANES), row_map),
                  pl.BlockSpec((tm, LANES), row_map),
                  _resident((1, d)), _resident(win.shape), _resident(pw.shape),
                  _resident((1, POOL_WIDTH)), _resident((1, ATTN_WIDTH)),
                  _resident((1, POOL_WIDTH)), _resident(wout.shape)],
        out_specs=pl.BlockSpec((tm, d), row_map),
        out_shape=jax.ShapeDtypeStruct((n, d), F32),
        scratch_shapes=[pltpu.VMEM((N_KV_HEADS, BLOCK + tm, LANES), BF16),
                        pltpu.VMEM((N_KV_HEADS, BLOCK + tm, LANES), BF16),
                        pltpu.VMEM((POOL_TOP + tm, POOL_WIDTH), F32),
                        pltpu.VMEM((POOL_TOP + tm, POOL_WIDTH), F32),
                        pltpu.VMEM((POOL_TOP + tm, POOL_WIDTH - POOL_GROUP_DIM), F32),
                        pltpu.VMEM((POOL_TOP + tm, POOL_WIDTH - 2 * POOL_GROUP_DIM), F32),
                        pltpu.VMEM((tm, ATTN_WIDTH), F32),
                        pltpu.VMEM((tm, POOL_WIDTH), F32)],
        compiler_params=pltpu.CompilerParams(
            dimension_semantics=("arbitrary", "arbitrary"), vmem_limit_bytes=VMEM_LIMIT),
        name="mixer",
    )(sinks, x, cos_t, sin_t, g, win, pw, ps, an, pn, wout)


def _mem_kv_kernel(mem_ref, g_ref, wkv_ref, kv_ref):
    mn = _rms(mem_ref[...], g_ref[...]).astype(BF16)
    kv_ref[...] = jnp.dot(mn, wkv_ref[...], preferred_element_type=F32).astype(BF16)


def _mem_kv(mem, g, wkv):
    b, m, d = mem.shape
    return pl.pallas_call(
        _mem_kv_kernel,
        grid=(b,),
        in_specs=[pl.BlockSpec((None, m, d), lambda i: (i, 0, 0)),
                  _resident((1, d)), _resident(wkv.shape)],
        out_specs=pl.BlockSpec((None, m, 2 * d), lambda i: (i, 0, 0)),
        out_shape=jax.ShapeDtypeStruct((b, m, 2 * d), BF16),
        compiler_params=pltpu.CompilerParams(
            dimension_semantics=("arbitrary",), vmem_limit_bytes=VMEM_LIMIT),
        name="mem_kv",
    )(mem, g, wkv)


def _xattn_body(x, g, wq_ref, k_ref, v_ref, wo_ref, att_ref):
    h = _rms(x, g).astype(BF16)
    q = jnp.dot(h, wq_ref[...], preferred_element_type=F32) * (X_HEAD_DIM ** -0.5)
    q = q.astype(BF16)
    for hd in range(X_HEADS):
        cols = slice(hd * X_HEAD_DIM, (hd + 1) * X_HEAD_DIM)
        s = lax.dot_general(k_ref[:, cols], q[:, cols], (((1,), (1,)), ((), ())),
                            preferred_element_type=F32)
        m = jnp.max(s, axis=0, keepdims=True)
        ex = jnp.exp(s - m)
        pr = (ex * (1.0 / jnp.sum(ex, axis=0, keepdims=True))).astype(BF16)
        att_ref[:, cols] = lax.dot_general(pr, v_ref[:, cols], (((0,), (0,)), ((), ())),
                                           preferred_element_type=F32).astype(BF16)
    return x + jnp.dot(att_ref[...], wo_ref[...], preferred_element_type=F32)


def _xattn_kernel(x_ref, g_ref, wq_ref, k_ref, v_ref, wo_ref, o_ref, att_ref):
    o_ref[...] = _xattn_body(x_ref[...], g_ref[...], wq_ref, k_ref, v_ref, wo_ref, att_ref)


def _xattn_ffn_kernel(x_ref, xg_ref, wq_ref, k_ref, v_ref, wo_ref,
                      fg_ref, wg_ref, wu_ref, wd_ref, final_g_ref, o_ref,
                      att_ref, mid_ref, *, final):
    @pl.when(pl.program_id(0) == 0)
    def _():
        mid_ref[...] = jnp.zeros(mid_ref.shape, F32)

    out = _ffn_body(mid_ref[...], fg_ref[...], wg_ref, wu_ref, wd_ref)
    if final:
        out = _rms(out, final_g_ref[...])
    o_ref[...] = out
    mid_ref[...] = _xattn_body(x_ref[...], xg_ref[...], wq_ref, k_ref, v_ref, wo_ref, att_ref)


def _xattn_ffn(x, batch, seq, xg, wq, kv, wo, fg, wg, wu, wd, final_g, final):
    n, d = x.shape
    tm = ROW_TILE
    spt = seq // tm
    nt = n // tm
    m = kv.shape[1]
    cur = lambda i: jnp.minimum(i, nt - 1)
    return pl.pallas_call(
        functools.partial(_xattn_ffn_kernel, final=final),
        grid=(nt + 1,),
        in_specs=[pl.BlockSpec((tm, d), lambda i: (cur(i), 0)),
                  _resident((1, d)), _resident(wq.shape),
                  pl.BlockSpec((None, m, d), lambda i: (cur(i) // spt, 0, 0)),
                  pl.BlockSpec((None, m, d), lambda i: (cur(i) // spt, 0, 1)),
                  _resident(wo.shape),
                  _resident((1, d)), _resident(wg.shape), _resident(wu.shape),
                  _resident(wd.shape), _resident((1, d))],
        out_specs=pl.BlockSpec((tm, d), lambda i: (jnp.maximum(i - 1, 0), 0)),
        out_shape=jax.ShapeDtypeStruct((n, d), F32),
        scratch_shapes=[pltpu.VMEM((tm, d), BF16), pltpu.VMEM((tm, d), F32)],
        compiler_params=pltpu.CompilerParams(
            dimension_semantics=("arbitrary",), vmem_limit_bytes=VMEM_LIMIT),
        name="xattn_ffn_final" if final else "xattn_ffn",
    )(x, xg, wq, kv, kv, wo, fg, wg, wu, wd, final_g)


def _xattn(x, batch, seq, g, wq, kv, wo):
    n, d = x.shape
    tm = ROW_TILE
    spt = seq // tm
    m = kv.shape[1]
    row_map = lambda b, j: (b * spt + j, 0)
    return pl.pallas_call(
        _xattn_kernel,
        grid=(batch, spt),
        in_specs=[pl.BlockSpec((tm, d), row_map),
                  _resident((1, d)), _resident(wq.shape),
                  pl.BlockSpec((None, m, d), lambda b, j: (b, 0, 0)),
                  pl.BlockSpec((None, m, d), lambda b, j: (b, 0, 1)),
                  _resident(wo.shape)],
        out_specs=pl.BlockSpec((tm, d), row_map),
        out_shape=jax.ShapeDtypeStruct((n, d), F32),
        scratch_shapes=[pltpu.VMEM((tm, d), BF16)],
        compiler_params=pltpu.CompilerParams(
            dimension_semantics=("arbitrary", "arbitrary"), vmem_limit_bytes=VMEM_LIMIT),
        name="xattn",
    )(x, g, wq, kv, kv, wo)


def kernel(x, mem, positions, ffn1_norm, ffn1_w_gate, ffn1_w_up, ffn1_w_down, mix_norm, w_in,
           attn_sinks, pool_w, pool_scale, attn_out_norm, pool_out_norm, w_out, xattn_norm,
           mem_norm, xattn_wq, xattn_wkv, xattn_wo, ffn2_norm, ffn2_w_gate, ffn2_w_up,
           ffn2_w_down, final_norm):
    batch, seq, d = x.shape
    depth = w_in.shape[0]
    assert seq % ROW_TILE == 0 and ROW_TILE % BLOCK == 0 and d == D_MODEL
    n = batch * seq
    bf = lambda w: w.astype(BF16)
    row = lambda v: v.reshape(1, -1)

    cos_t, sin_t = _rope_tables(positions)
    xf = x.reshape(n, d)
    fg = row(final_norm)
    for l in range(depth):
        xf = _ffn(xf, row(ffn1_norm[l]), bf(ffn1_w_gate[l]), bf(ffn1_w_up[l]),
                  bf(ffn1_w_down[l]), fg, final=False)
        xf = _mixer(xf, cos_t, sin_t, batch, seq, attn_sinks[l], row(mix_norm[l]), bf(w_in[l]),
                    bf(pool_w[l]), row(pool_scale[l]), row(attn_out_norm[l]),
                    row(pool_out_norm[l]), bf(w_out[l]))
        kv = _mem_kv(mem, row(mem_norm[l]), bf(xattn_wkv[l]))
        xf = _xattn(xf, batch, seq, row(xattn_norm[l]), bf(xattn_wq[l]), kv, bf(xattn_wo[l]))
        xf = _ffn(xf, row(ffn2_norm[l]), bf(ffn2_w_gate[l]), bf(ffn2_w_up[l]),
                  bf(ffn2_w_down[l]), fg, final=(l == depth - 1))
    return xf.reshape(batch, seq, d)
```

```python
import functools

import jax
import jax.numpy as jnp
from jax import lax
from jax.experimental import pallas as pl
from jax.experimental.pallas import tpu as pltpu

F32 = jnp.float32
BF16 = jnp.bfloat16

D_MODEL = 1024
HEAD_DIM = 64
N_Q_HEADS = 8
N_KV_HEADS = 2
ATTN_WIDTH = N_Q_HEADS * HEAD_DIM
KV_WIDTH = N_KV_HEADS * HEAD_DIM
BLOCK = 128
POOL_WINDOWS = (2, 4, 8, 16)
POOL_GROUP_DIM = 128
POOL_WIDTH = 512
POOL_HALO = 16
POOL_TOP = 32
X_HEADS = 4
X_HEAD_DIM = 256
D_FF = 2816
FFN_RES = 0.5
EPS = 1e-6
ROPE_THETA = 10000.0
NEG = -1e30
LANES = 128
BF16_SUBLANES = 16

ROW_TILE = 1024
MXU_TILE = 256
FF_CHUNKS = tuple((lo, min(lo + 3 * MXU_TILE, D_FF)) for lo in range(0, D_FF, 3 * MXU_TILE))
VMEM_LIMIT = 56 * 1024 * 1024


def _rms(x, g):
    ms = jnp.mean(x * x, axis=-1, keepdims=True)
    return x * lax.rsqrt(ms + EPS) * g


def _resident(shape):
    nd = len(shape)
    return pl.BlockSpec(shape, lambda *_: (0,) * nd, pipeline_mode=pl.Buffered(1))


def _rope_table_kernel(pos_ref, freq_ref, cos_ref, sin_ref):
    ang = pos_ref[...].astype(F32) * freq_ref[...]
    lane = lax.broadcasted_iota(jnp.int32, ang.shape, 1)
    first_half = (lane % HEAD_DIM) < (HEAD_DIM // 2)
    cos_ref[...] = jnp.cos(ang)
    s = jnp.sin(ang)
    sin_ref[...] = jnp.where(first_half, -s, s)


def _rope_tables(positions):
    n = positions.size
    inv_freq = ROPE_THETA ** (-jnp.arange(0, HEAD_DIM, 2, dtype=F32) / HEAD_DIM)
    freq = jnp.tile(inv_freq, LANES // (HEAD_DIM // 2)).reshape(1, LANES)
    tm = 2048
    return pl.pallas_call(
        _rope_table_kernel,
        grid=(n // tm,),
        in_specs=[pl.BlockSpec((tm, 1), lambda i: (i, 0)),
                  pl.BlockSpec((1, LANES), lambda i: (0, 0))],
        out_specs=[pl.BlockSpec((tm, LANES), lambda i: (i, 0))] * 2,
        out_shape=[jax.ShapeDtypeStruct((n, LANES), F32)] * 2,
        name="rope_tables",
    )(positions.reshape(n, 1), freq)


def _ffn_body(x, g, wg_ref, wu_ref, wd_ref):
    xn = _rms(x, g).astype(BF16)
    y = None
    for lo, hi in FF_CHUNKS:
        gate = jnp.dot(xn, wg_ref[:, lo:hi], preferred_element_type=F32)
        up = jnp.dot(xn, wu_ref[:, lo:hi], preferred_element_type=F32)
        h = (gate / (1.0 + jnp.exp(-gate)) * up).astype(BF16)
        part = jnp.dot(h, wd_ref[lo:hi, :], preferred_element_type=F32)
        y = part if y is None else y + part
    return x + FFN_RES * y


def _cast_blocks(rows, nsteps):
    rb = -(-rows // nsteps)
    rb = -(-rb // BF16_SUBLANES) * BF16_SUBLANES
    while rows % rb:
        rb += BF16_SUBLANES
    return rb, rows // rb


def _ffn_kernel(*refs, final, n_casts):
    x_ref, g_ref, wg_ref, wu_ref, wd_ref, fg_ref = refs[:6]
    cast_in, (o_ref, *cast_out) = refs[6:6 + n_casts], refs[6 + n_casts:]
    out = _ffn_body(x_ref[...], g_ref[...], wg_ref, wu_ref, wd_ref)
    if final:
        out = _rms(out, fg_ref[...])
    o_ref[...] = out
    for src, dst in zip(cast_in, cast_out):
        dst[...] = src[...].astype(BF16)


def _ffn(x, g, wg, wu, wd, fg, final, casts=()):
    n, d = x.shape
    tm = ROW_TILE
    nsteps = n // tm
    in_specs = [pl.BlockSpec((tm, d), lambda i: (i, 0)),
                _resident((1, d)), _resident(wg.shape), _resident(wu.shape),
                _resident(wd.shape), _resident((1, d))]
    out_specs = [pl.BlockSpec((tm, d), lambda i: (i, 0))]
    out_shape = [jax.ShapeDtypeStruct((n, d), F32)]
    args = [x, g, wg, wu, wd, fg]
    for src, layer in casts:
        rows, cols = src.shape[1:]
        rb, nb = _cast_blocks(rows, nsteps)
        in_specs.append(pl.BlockSpec(
            (None, rb, cols), lambda i, layer=layer, nb=nb: (layer, jnp.minimum(i, nb - 1), 0)))
        out_specs.append(pl.BlockSpec((rb, cols), lambda i, nb=nb: (jnp.minimum(i, nb - 1), 0)))
        out_shape.append(jax.ShapeDtypeStruct((rows, cols), BF16))
        args.append(src)
    res = pl.pallas_call(
        functools.partial(_ffn_kernel, final=final, n_casts=len(casts)),
        grid=(nsteps,),
        in_specs=in_specs,
        out_specs=out_specs,
        out_shape=out_shape,
        compiler_params=pltpu.CompilerParams(
            dimension_semantics=("arbitrary",), vmem_limit_bytes=VMEM_LIMIT),
        name="ffn_final" if final else "ffn",
    )(*args)
    return res[0], res[1:]


def _rope(t, cos, sin_signed):
    lane = lax.broadcasted_iota(jnp.int32, t.shape, 1)
    first_half = (lane % HEAD_DIM) < (HEAD_DIM // 2)
    swapped = jnp.where(first_half,
                        pltpu.roll(t, LANES - HEAD_DIM // 2, 1),
                        pltpu.roll(t, HEAD_DIM // 2, 1))
    return t * cos + swapped * sin_signed


def _mixer_kernel(sinks_ref, x_ref, cos_ref, sin_ref, g_ref, win_ref, pw_ref, ps_ref,
                  an_ref, pn_ref, wout_ref, o_ref,
                  kd, vd, ub, l1, l2, l3, oa_ref, ob_ref):
    j = pl.program_id(1)
    tm = x_ref.shape[0]
    nblk = tm // BLOCK
    group = N_Q_HEADS // N_KV_HEADS

    @pl.when(j == 0)
    def _():
        kd[:, 0:BLOCK, :] = jnp.zeros((N_KV_HEADS, BLOCK, LANES), BF16)
        vd[:, 0:BLOCK, :] = jnp.zeros((N_KV_HEADS, BLOCK, LANES), BF16)
        ub[0:POOL_TOP, :] = jnp.zeros((POOL_TOP, POOL_WIDTH), F32)

    x = x_ref[...]
    h = _rms(x, g_ref[...]).astype(BF16)
    proj = jnp.dot(h, win_ref[...], preferred_element_type=F32)
    cos = cos_ref[...]
    sin = sin_ref[...]

    low_t = lax.broadcasted_iota(jnp.int32, (tm, LANES), 1) < HEAD_DIM
    k = _rope(proj[:, ATTN_WIDTH:ATTN_WIDTH + KV_WIDTH], cos, sin)
    v = proj[:, ATTN_WIDTH + KV_WIDTH:ATTN_WIDTH + 2 * KV_WIDTH]
    kr = pltpu.roll(k, HEAD_DIM, 1)
    vr = pltpu.roll(v, HEAD_DIM, 1)
    kd[0, BLOCK:BLOCK + tm, :] = jnp.where(low_t, k, kr).astype(BF16)
    kd[1, BLOCK:BLOCK + tm, :] = jnp.where(low_t, kr, k).astype(BF16)
    vd[0, BLOCK:BLOCK + tm, :] = jnp.where(low_t, v, vr).astype(BF16)
    vd[1, BLOCK:BLOCK + tm, :] = jnp.where(low_t, vr, v).astype(BF16)
    ub[POOL_TOP:POOL_TOP + tm, :] = proj[:, ATTN_WIDTH + 2 * KV_WIDTH:]

    low = lax.broadcasted_iota(jnp.int32, (BLOCK, LANES), 1) < HEAD_DIM
    tri_t = (lax.broadcasted_iota(jnp.int32, (BLOCK, BLOCK), 0)
             <= lax.broadcasted_iota(jnp.int32, (BLOCK, BLOCK), 1))
    tri_tf = tri_t.astype(F32)
    first = j == 0
    scale = HEAD_DIM ** -0.5
    zero_bf = jnp.zeros((BLOCK, LANES), BF16)

    q_pairs = [(_rope(proj[:, p * LANES:(p + 1) * LANES], cos, sin) * scale).astype(BF16)
               for p in range(ATTN_WIDTH // LANES)]
    for kvh in range(N_KV_HEADS):
        for b in range(nblk):
            rows = slice(b * BLOCK, (b + 1) * BLOCK)
            krows = slice(b * BLOCK, b * BLOCK + 2 * BLOCK)
            pieces = []
            for idx in range(group):
                qb = q_pairs[(kvh * group + idx) // 2][rows, :]
                pieces.append(jnp.where(low if idx % 2 == 0 else ~low, qb, zero_bf))
            q4 = jnp.concatenate(pieces, axis=0)
            s4 = lax.dot_general(kd[kvh, krows, :], q4, (((1,), (1,)), ((), ())),
                                 preferred_element_type=F32)
            probs = []
            for idx in range(group):
                hcols = slice(idx * BLOCK, (idx + 1) * BLOCK)
                s_prev = s4[0:BLOCK, hcols]
                if b == 0:
                    s_prev = jnp.where(first, NEG, s_prev)
                s = jnp.where(tri_t, s4[BLOCK:2 * BLOCK, hcols], s_prev)
                sink = sinks_ref[kvh * group + idx]
                m = jnp.maximum(jnp.max(s, axis=0, keepdims=True), sink)
                ex = jnp.exp(s - m)
                den = jnp.sum(ex, axis=0, keepdims=True) + jnp.exp(sink - m)
                pr = ex * (1.0 / den)
                p_cur = pr * tri_tf
                probs.append(jnp.concatenate([pr - p_cur, p_cur], axis=0).astype(BF16))
            p4 = jnp.concatenate(probs, axis=1)
            o4 = lax.dot_general(p4, vd[kvh, krows, :], (((0,), (0,)), ((), ())),
                                 preferred_element_type=F32)
            for pp in range(group // 2):
                pcol = (kvh * (group // 2) + pp) * LANES
                oa_ref[rows, pcol:pcol + LANES] = jnp.where(
                    low, o4[(2 * pp) * BLOCK:(2 * pp + 1) * BLOCK, :],
                    o4[(2 * pp + 1) * BLOCK:(2 * pp + 2) * BLOCK, :])

    n1 = tm + POOL_TOP - 8
    l1[8:8 + n1, :] = ub[8:8 + n1, :] + ub[7:7 + n1, :]
    n2 = tm + POOL_TOP - 16
    c2 = POOL_GROUP_DIM
    l2[16:16 + n2, :] = l1[16:16 + n2, c2:] + l1[14:14 + n2, c2:]
    n3 = tm + POOL_TOP - 24
    l3[24:24 + n3, :] = l2[24:24 + n3, c2:] + l2[20:20 + n3, c2:]
    top = POOL_TOP
    sums = [l1[top:top + tm, 0:c2], l2[top:top + tm, 0:c2], l3[top:top + tm, 0:c2],
            l3[top:top + tm, c2:] + l3[top - 8:top - 8 + tm, c2:]]
    row = lax.broadcasted_iota(jnp.int32, (tm, POOL_GROUP_DIM), 0) + j * tm
    for gi, w in enumerate(POOL_WINDOWS):
        cols = slice(gi * POOL_GROUP_DIM, (gi + 1) * POOL_GROUP_DIM)
        cnt = jnp.minimum(row + 1, w).astype(F32)
        pooled = (sums[gi] / cnt - ub[top:top + tm, cols]).astype(BF16)
        mixed = jnp.dot(pooled, pw_ref[cols, :], preferred_element_type=F32)
        ob_ref[:, cols] = mixed * ps_ref[:, cols]

    kd[:, 0:BLOCK, :] = kd[:, tm:tm + BLOCK, :]
    vd[:, 0:BLOCK, :] = vd[:, tm:tm + BLOCK, :]
    ub[POOL_TOP - POOL_HALO:POOL_TOP, :] = ub[tm + POOL_TOP - POOL_HALO:tm + POOL_TOP, :]

    na = _rms(oa_ref[...], an_ref[...]).astype(BF16)
    nb = _rms(ob_ref[...], pn_ref[...]).astype(BF16)
    y = jnp.dot(na, wout_ref[0:ATTN_WIDTH, :], preferred_element_type=F32)
    y = y + jnp.dot(nb, wout_ref[ATTN_WIDTH:, :], preferred_element_type=F32)
    o_ref[...] = x + y


def _mixer(x, cos_t, sin_t, batch, seq, sinks, g, win, pw, ps, an, pn, wout):
    n, d = x.shape
    tm = ROW_TILE
    spt = seq // tm
    row_map = lambda b, j: (b * spt + j, 0)
    return pl.pallas_call(
        _mixer_kernel,
        grid=(batch, spt),
        in_specs=[pl.BlockSpec(memory_space=pltpu.SMEM),
                  pl.BlockSpec((tm, d), row_map),
                  pl.BlockSpec((tm, LANES), row_map),
                  pl.BlockSpec((tm, LANES), row_map),
                  _resident((1, d)), _resident(win.shape), _resident(pw.shape),
                  _resident((1, POOL_WIDTH)), _resident((1, ATTN_WIDTH)),
                  _resident((1, POOL_WIDTH)), _resident(wout.shape)],
        out_specs=pl.BlockSpec((tm, d), row_map),
        out_shape=jax.ShapeDtypeStruct((n, d), F32),
        scratch_shapes=[pltpu.VMEM((N_KV_HEADS, BLOCK + tm, LANES), BF16),
                        pltpu.VMEM((N_KV_HEADS, BLOCK + tm, LANES), BF16),
                        pltpu.VMEM((POOL_TOP + tm, POOL_WIDTH), F32),
                        pltpu.VMEM((POOL_TOP + tm, POOL_WIDTH), F32),
                        pltpu.VMEM((POOL_TOP + tm, POOL_WIDTH - POOL_GROUP_DIM), F32),
                        pltpu.VMEM((POOL_TOP + tm, POOL_WIDTH - 2 * POOL_GROUP_DIM), F32),
                        pltpu.VMEM((tm, ATTN_WIDTH), F32),
                        pltpu.VMEM((tm, POOL_WIDTH), F32)],
        compiler_params=pltpu.CompilerParams(
            dimension_semantics=("arbitrary", "arbitrary"), vmem_limit_bytes=VMEM_LIMIT),
        name="mixer",
    )(sinks, x, cos_t, sin_t, g, win, pw, ps, an, pn, wout)


def _mem_kv_kernel(mem_ref, g_ref, wkv_ref, kv_ref):
    mn = _rms(mem_ref[...], g_ref[...]).astype(BF16)
    kv_ref[...] = jnp.dot(mn, wkv_ref[...], preferred_element_type=F32).astype(BF16)


def _mem_kv(mem, g, wkv):
    b, m, d = mem.shape
    return pl.pallas_call(
        _mem_kv_kernel,
        grid=(b,),
        in_specs=[pl.BlockSpec((None, m, d), lambda i: (i, 0, 0)),
                  _resident((1, d)), _resident(wkv.shape)],
        out_specs=pl.BlockSpec((None, m, 2 * d), lambda i: (i, 0, 0)),
        out_shape=jax.ShapeDtypeStruct((b, m, 2 * d), BF16),
        compiler_params=pltpu.CompilerParams(
            dimension_semantics=("arbitrary",), vmem_limit_bytes=VMEM_LIMIT),
        name="mem_kv",
    )(mem, g, wkv)


def _xattn_kernel(x_ref, g_ref, wq_ref, k_ref, v_ref, wo_ref, o_ref, att_ref):
    x = x_ref[...]
    h = _rms(x, g_ref[...]).astype(BF16)
    q = jnp.dot(h, wq_ref[...], preferred_element_type=F32) * (X_HEAD_DIM ** -0.5)
    q = q.astype(BF16)
    for hd in range(X_HEADS):
        cols = slice(hd * X_HEAD_DIM, (hd + 1) * X_HEAD_DIM)
        s = lax.dot_general(k_ref[:, cols], q[:, cols], (((1,), (1,)), ((), ())),
                            preferred_element_type=F32)
        m = jnp.max(s, axis=0, keepdims=True)
        ex = jnp.exp(s - m)
        pr = (ex * (1.0 / jnp.sum(ex, axis=0, keepdims=True))).astype(BF16)
        att_ref[:, cols] = lax.dot_general(pr, v_ref[:, cols], (((0,), (0,)), ((), ())),
                                           preferred_element_type=F32).astype(BF16)
    o_ref[...] = x + jnp.dot(att_ref[...], wo_ref[...], preferred_element_type=F32)


def _xattn(x, batch, seq, g, wq, kv, wo):
    n, d = x.shape
    tm = ROW_TILE
    spt = seq // tm
    m = kv.shape[1]
    row_map = lambda b, j: (b * spt + j, 0)
    return pl.pallas_call(
        _xattn_kernel,
        grid=(batch, spt),
        in_specs=[pl.BlockSpec((tm, d), row_map),
                  _resident((1, d)), _resident(wq.shape),
                  pl.BlockSpec((None, m, d), lambda b, j: (b, 0, 0)),
                  pl.BlockSpec((None, m, d), lambda b, j: (b, 0, 1)),
                  _resident(wo.shape)],
        out_specs=pl.BlockSpec((tm, d), row_map),
        out_shape=jax.ShapeDtypeStruct((n, d), F32),
        scratch_shapes=[pltpu.VMEM((tm, d), BF16)],
        compiler_params=pltpu.CompilerParams(
            dimension_semantics=("arbitrary", "arbitrary"), vmem_limit_bytes=VMEM_LIMIT),
        name="xattn",
    )(x, g, wq, kv, kv, wo)


def kernel(x, mem, positions, ffn1_norm, ffn1_w_gate, ffn1_w_up, ffn1_w_down, mix_norm, w_in,
           attn_sinks, pool_w, pool_scale, attn_out_norm, pool_out_norm, w_out, xattn_norm,
           mem_norm, xattn_wq, xattn_wkv, xattn_wo, ffn2_norm, ffn2_w_gate, ffn2_w_up,
           ffn2_w_down, final_norm):
    batch, seq, d = x.shape
    depth = w_in.shape[0]
    assert seq % ROW_TILE == 0 and ROW_TILE % BLOCK == 0 and d == D_MODEL
    n = batch * seq
    row = lambda v: v.reshape(1, -1)
    pool_w2 = pool_w.reshape(depth, POOL_WIDTH, POOL_GROUP_DIM)
    ffn1_w = (ffn1_w_gate, ffn1_w_up, ffn1_w_down)
    ffn2_w = (ffn2_w_gate, ffn2_w_up, ffn2_w_down)
    rest_w = (w_in, pool_w2, w_out, xattn_wq, xattn_wkv, xattn_wo) + ffn2_w

    xf = x.reshape(n, d)
    fg = row(final_norm)
    w1 = [w[0].astype(BF16) for w in ffn1_w]
    cos_t, sin_t = _rope_tables(positions)
    for l in range(depth):
        xf, wl = _ffn(xf, row(ffn1_norm[l]), *w1, fg, final=False,
                      casts=[(w, l) for w in rest_w])
        win_b, pw_b, wout_b, wq_b, wkv_b, wo_b = wl[:6]
        xf = _mixer(xf, cos_t, sin_t, batch, seq, attn_sinks[l], row(mix_norm[l]), win_b,
                    pw_b, row(pool_scale[l]), row(attn_out_norm[l]), row(pool_out_norm[l]), wout_b)
        kv = _mem_kv(mem, row(mem_norm[l]), wkv_b)
        xf = _xattn(xf, batch, seq, row(xattn_norm[l]), wq_b, kv, wo_b)
        last = l == depth - 1
        xf, w1 = _ffn(xf, row(ffn2_norm[l]), *wl[6:], fg, final=last,
                      casts=[] if last else [(w, l + 1) for w in ffn1_w])
    return xf.reshape(batch, seq, d)
```

```python
import functools

import jax
import jax.numpy as jnp
from jax import lax
from jax.experimental import pallas as pl
from jax.experimental.pallas import tpu as pltpu

F32 = jnp.float32
BF16 = jnp.bfloat16

D_MODEL = 1024
HEAD_DIM = 64
N_Q_HEADS = 8
N_KV_HEADS = 2
ATTN_WIDTH = N_Q_HEADS * HEAD_DIM
KV_WIDTH = N_KV_HEADS * HEAD_DIM
BLOCK = 128
POOL_WINDOWS = (2, 4, 8, 16)
POOL_GROUP_DIM = 128
POOL_WIDTH = 512
POOL_HALO = 16
POOL_TOP = 32
X_HEADS = 4
X_HEAD_DIM = 256
D_FF = 2816
FFN_RES = 0.5
EPS = 1e-6
ROPE_THETA = 10000.0
NEG = -1e30
LANES = 128
BF16_SUBLANES = 16

ROW_TILE = 1024
MXU_TILE = 256
FF_CHUNKS = tuple((lo, min(lo + 3 * MXU_TILE, D_FF)) for lo in range(0, D_FF, 3 * MXU_TILE))
VMEM_LIMIT = 56 * 1024 * 1024


def _rms(x, g):
    ms = jnp.mean(x * x, axis=-1, keepdims=True)
    return x * lax.rsqrt(ms + EPS) * g


def _resident(shape):
    nd = len(shape)
    return pl.BlockSpec(shape, lambda *_: (0,) * nd, pipeline_mode=pl.Buffered(1))


def _rope_table_kernel(pos_ref, freq_ref, cos_ref, sin_ref):
    ang = pos_ref[...].astype(F32) * freq_ref[...]
    lane = lax.broadcasted_iota(jnp.int32, ang.shape, 1)
    first_half = (lane % HEAD_DIM) < (HEAD_DIM // 2)
    cos_ref[...] = jnp.cos(ang)
    s = jnp.sin(ang)
    sin_ref[...] = jnp.where(first_half, -s, s)


def _rope_tables(positions):
    n = positions.size
    inv_freq = ROPE_THETA ** (-jnp.arange(0, HEAD_DIM, 2, dtype=F32) / HEAD_DIM)
    freq = jnp.tile(inv_freq, LANES // (HEAD_DIM // 2)).reshape(1, LANES)
    tm = 2048
    return pl.pallas_call(
        _rope_table_kernel,
        grid=(n // tm,),
        in_specs=[pl.BlockSpec((tm, 1), lambda i: (i, 0)),
                  pl.BlockSpec((1, LANES), lambda i: (0, 0))],
        out_specs=[pl.BlockSpec((tm, LANES), lambda i: (i, 0))] * 2,
        out_shape=[jax.ShapeDtypeStruct((n, LANES), F32)] * 2,
        name="rope_tables",
    )(positions.reshape(n, 1), freq)


def _ffn_body(x, g, wg_ref, wu_ref, wd_ref):
    xn = _rms(x, g).astype(BF16)
    y = None
    for lo, hi in FF_CHUNKS:
        gate = jnp.dot(xn, wg_ref[:, lo:hi], preferred_element_type=F32)
        up = jnp.dot(xn, wu_ref[:, lo:hi], preferred_element_type=F32)
        h = (gate / (1.0 + jnp.exp(-gate)) * up).astype(BF16)
        part = jnp.dot(h, wd_ref[lo:hi, :], preferred_element_type=F32)
        y = part if y is None else y + part
    return x + FFN_RES * y


def _cast_blocks(rows, nsteps):
    rb = -(-rows // nsteps)
    rb = -(-rb // BF16_SUBLANES) * BF16_SUBLANES
    while rows % rb:
        rb += BF16_SUBLANES
    return rb, rows // rb


def _ffn_kernel(*refs, final, n_casts):
    x_ref, g_ref, wg_ref, wu_ref, wd_ref, fg_ref = refs[:6]
    cast_in, (o_ref, *cast_out) = refs[6:6 + n_casts], refs[6 + n_casts:]
    out = _ffn_body(x_ref[...], g_ref[...], wg_ref, wu_ref, wd_ref)
    if final:
        out = _rms(out, fg_ref[...])
    o_ref[...] = out
    for src, dst in zip(cast_in, cast_out):
        dst[...] = src[...].astype(BF16)


def _ffn(x, g, wg, wu, wd, fg, final, casts=()):
    n, d = x.shape
    tm = ROW_TILE
    nsteps = n // tm
    in_specs = [pl.BlockSpec((tm, d), lambda i: (i, 0)),
                _resident((1, d)), _resident(wg.shape), _resident(wu.shape),
                _resident(wd.shape), _resident((1, d))]
    out_specs = [pl.BlockSpec((tm, d), lambda i: (i, 0))]
    out_shape = [jax.ShapeDtypeStruct((n, d), F32)]
    args = [x, g, wg, wu, wd, fg]
    for src, layer in casts:
        rows, cols = src.shape[1:]
        rb, nb = _cast_blocks(rows, nsteps)
        in_specs.append(pl.BlockSpec(
            (None, rb, cols), lambda i, layer=layer, nb=nb: (layer, jnp.minimum(i, nb - 1), 0)))
        out_specs.append(pl.BlockSpec((rb, cols), lambda i, nb=nb: (jnp.minimum(i, nb - 1), 0)))
        out_shape.append(jax.ShapeDtypeStruct((rows, cols), BF16))
        args.append(src)
    res = pl.pallas_call(
        functools.partial(_ffn_kernel, final=final, n_casts=len(casts)),
        grid=(nsteps,),
        in_specs=in_specs,
        out_specs=out_specs,
        out_shape=out_shape,
        compiler_params=pltpu.CompilerParams(
            dimension_semantics=("arbitrary",), vmem_limit_bytes=VMEM_LIMIT),
        name="ffn_final" if final else "ffn",
    )(*args)
    return res[0], res[1:]


def _rope(t, cos, sin_signed):
    lane = lax.broadcasted_iota(jnp.int32, t.shape, 1)
    first_half = (lane % HEAD_DIM) < (HEAD_DIM // 2)
    swapped = jnp.where(first_half,
                        pltpu.roll(t, LANES - HEAD_DIM // 2, 1),
                        pltpu.roll(t, HEAD_DIM // 2, 1))
    return t * cos + swapped * sin_signed


def _mixer_kernel(sinks_ref, x_ref, cos_ref, sin_ref, g_ref, win_ref, pw_ref, ps_ref,
                  an_ref, pn_ref, wout_ref, o_ref,
                  kd, vd, ub, l1, l2, l3, oa_ref, ob_ref):
    j = pl.program_id(1)
    tm = x_ref.shape[0]
    nblk = tm // BLOCK
    group = N_Q_HEADS // N_KV_HEADS

    @pl.when(j == 0)
    def _():
        kd[:, 0:BLOCK, :] = jnp.zeros((N_KV_HEADS, BLOCK, LANES), BF16)
        vd[:, 0:BLOCK, :] = jnp.zeros((N_KV_HEADS, BLOCK, LANES), BF16)
        ub[0:POOL_TOP, :] = jnp.zeros((POOL_TOP, POOL_WIDTH), F32)

    x = x_ref[...]
    h = _rms(x, g_ref[...]).astype(BF16)
    proj = jnp.dot(h, win_ref[...], preferred_element_type=F32)
    cos = cos_ref[...]
    sin = sin_ref[...]

    low_t = lax.broadcasted_iota(jnp.int32, (tm, LANES), 1) < HEAD_DIM
    k = _rope(proj[:, ATTN_WIDTH:ATTN_WIDTH + KV_WIDTH], cos, sin)
    v = proj[:, ATTN_WIDTH + KV_WIDTH:ATTN_WIDTH + 2 * KV_WIDTH]
    kr = pltpu.roll(k, HEAD_DIM, 1)
    vr = pltpu.roll(v, HEAD_DIM, 1)
    kd[0, BLOCK:BLOCK + tm, :] = jnp.where(low_t, k, kr).astype(BF16)
    kd[1, BLOCK:BLOCK + tm, :] = jnp.where(low_t, kr, k).astype(BF16)
    vd[0, BLOCK:BLOCK + tm, :] = jnp.where(low_t, v, vr).astype(BF16)
    vd[1, BLOCK:BLOCK + tm, :] = jnp.where(low_t, vr, v).astype(BF16)
    ub[POOL_TOP:POOL_TOP + tm, :] = proj[:, ATTN_WIDTH + 2 * KV_WIDTH:]

    low = lax.broadcasted_iota(jnp.int32, (BLOCK, LANES), 1) < HEAD_DIM
    tri_t = (lax.broadcasted_iota(jnp.int32, (BLOCK, BLOCK), 0)
             <= lax.broadcasted_iota(jnp.int32, (BLOCK, BLOCK), 1))
    tri_tf = tri_t.astype(F32)
    first = j == 0
    scale = HEAD_DIM ** -0.5
    zero_bf = jnp.zeros((BLOCK, LANES), BF16)

    q_pairs = [(_rope(proj[:, p * LANES:(p + 1) * LANES], cos, sin) * scale).astype(BF16)
               for p in range(ATTN_WIDTH // LANES)]
    def scores(kvh, b):
        rows = slice(b * BLOCK, (b + 1) * BLOCK)
        pieces = []
        for idx in range(group):
            qb = q_pairs[(kvh * group + idx) // 2][rows, :]
            pieces.append(jnp.where(low if idx % 2 == 0 else ~low, qb, zero_bf))
        q4 = jnp.concatenate(pieces, axis=0)
        return lax.dot_general(kd[kvh, b * BLOCK:(b + 2) * BLOCK, :], q4, (((1,), (1,)), ((), ())),
                               preferred_element_type=F32)

    def attend(kvh, b, s4):
        rows = slice(b * BLOCK, (b + 1) * BLOCK)
        probs = []
        for idx in range(group):
            hcols = slice(idx * BLOCK, (idx + 1) * BLOCK)
            s_prev = s4[0:BLOCK, hcols]
            if b == 0:
                s_prev = jnp.where(first, NEG, s_prev)
            s = jnp.where(tri_t, s4[BLOCK:2 * BLOCK, hcols], s_prev)
            sink = sinks_ref[kvh * group + idx]
            m = jnp.maximum(jnp.max(s, axis=0, keepdims=True), sink)
            ex = jnp.exp(s - m)
            den = jnp.sum(ex, axis=0, keepdims=True) + jnp.exp(sink - m)
            pr = ex * (1.0 / den)
            p_cur = pr * tri_tf
            probs.append(jnp.concatenate([pr - p_cur, p_cur], axis=0).astype(BF16))
        p4 = jnp.concatenate(probs, axis=1)
        o4 = lax.dot_general(p4, vd[kvh, b * BLOCK:(b + 2) * BLOCK, :], (((0,), (0,)), ((), ())),
                             preferred_element_type=F32)
        for pp in range(group // 2):
            pcol = (kvh * (group // 2) + pp) * LANES
            oa_ref[rows, pcol:pcol + LANES] = jnp.where(
                low, o4[(2 * pp) * BLOCK:(2 * pp + 1) * BLOCK, :],
                o4[(2 * pp + 1) * BLOCK:(2 * pp + 2) * BLOCK, :])

    units = [(kvh, b) for kvh in range(N_KV_HEADS) for b in range(nblk)]
    s_next = scores(*units[0])
    for u, unit in enumerate(units):
        s_cur = s_next
        if u + 1 < len(units):
            s_next = scores(*units[u + 1])
        attend(*unit, s_cur)

    n1 = tm + POOL_TOP - 8
    l1[8:8 + n1, :] = ub[8:8 + n1, :] + ub[7:7 + n1, :]
    n2 = tm + POOL_TOP - 16
    c2 = POOL_GROUP_DIM
    l2[16:16 + n2, :] = l1[16:16 + n2, c2:] + l1[14:14 + n2, c2:]
    n3 = tm + POOL_TOP - 24
    l3[24:24 + n3, :] = l2[24:24 + n3, c2:] + l2[20:20 + n3, c2:]
    top = POOL_TOP
    sums = [l1[top:top + tm, 0:c2], l2[top:top + tm, 0:c2], l3[top:top + tm, 0:c2],
            l3[top:top + tm, c2:] + l3[top - 8:top - 8 + tm, c2:]]
    row = lax.broadcasted_iota(jnp.int32, (tm, POOL_GROUP_DIM), 0) + j * tm
    for gi, w in enumerate(POOL_WINDOWS):
        cols = slice(gi * POOL_GROUP_DIM, (gi + 1) * POOL_GROUP_DIM)
        cnt = jnp.minimum(row + 1, w).astype(F32)
        pooled = (sums[gi] / cnt - ub[top:top + tm, cols]).astype(BF16)
        mixed = jnp.dot(pooled, pw_ref[cols, :], preferred_element_type=F32)
        ob_ref[:, cols] = mixed * ps_ref[:, cols]

    kd[:, 0:BLOCK, :] = kd[:, tm:tm + BLOCK, :]
    vd[:, 0:BLOCK, :] = vd[:, tm:tm + BLOCK, :]
    ub[POOL_TOP - POOL_HALO:POOL_TOP, :] = ub[tm + POOL_TOP - POOL_HALO:tm + POOL_TOP, :]

    na = _rms(oa_ref[...], an_ref[...]).astype(BF16)
    nb = _rms(ob_ref[...], pn_ref[...]).astype(BF16)
    y = jnp.dot(na, wout_ref[0:ATTN_WIDTH, :], preferred_element_type=F32)
    y = y + jnp.dot(nb, wout_ref[ATTN_WIDTH:, :], preferred_element_type=F32)
    o_ref[...] = x + y


def _mixer(x, cos_t, sin_t, batch, seq, sinks, g, win, pw, ps, an, pn, wout):
    n, d = x.shape
    tm = ROW_TILE
    spt = seq // tm
    row_map = lambda b, j: (b * spt + j, 0)
    return pl.pallas_call(
        _mixer_kernel,
        grid=(batch, spt),
        in_specs=[pl.BlockSpec(memory_space=pltpu.SMEM),
                  pl.BlockSpec((tm, d), row_map),
                  pl.BlockSpec((tm, LANES), row_map),
                  pl.BlockSpec((tm, LANES), row_map),
                  _resident((1, d)), _resident(win.shape), _resident(pw.shape),
                  _resident((1, POOL_WIDTH)), _resident((1, ATTN_WIDTH)),
                  _resident((1, POOL_WIDTH)), _resident(wout.shape)],
        out_specs=pl.BlockSpec((tm, d), row_map),
        out_shape=jax.ShapeDtypeStruct((n, d), F32),
        scratch_shapes=[pltpu.VMEM((N_KV_HEADS, BLOCK + tm, LANES), BF16),
                        pltpu.VMEM((N_KV_HEADS, BLOCK + tm, LANES), BF16),
                        pltpu.VMEM((POOL_TOP + tm, POOL_WIDTH), F32),
                        pltpu.VMEM((POOL_TOP + tm, POOL_WIDTH), F32),
                        pltpu.VMEM((POOL_TOP + tm, POOL_WIDTH - POOL_GROUP_DIM), F32),
                        pltpu.VMEM((POOL_TOP + tm, POOL_WIDTH - 2 * POOL_GROUP_DIM), F32),
                        pltpu.VMEM((tm, ATTN_WIDTH), F32),
                        pltpu.VMEM((tm, POOL_WIDTH), F32)],
        compiler_params=pltpu.CompilerParams(
            dimension_semantics=("arbitrary", "arbitrary"), vmem_limit_bytes=VMEM_LIMIT),
        name="mixer",
    )(sinks, x, cos_t, sin_t, g, win, pw, ps, an, pn, wout)


def _mem_kv_kernel(mem_ref, g_ref, wkv_ref, kv_ref):
    mn = _rms(mem_ref[...], g_ref[...]).astype(BF16)
    kv_ref[...] = jnp.dot(mn, wkv_ref[...], preferred_element_type=F32).astype(BF16)


def _mem_kv(mem, g, wkv):
    b, m, d = mem.shape
    return pl.pallas_call(
        _mem_kv_kernel,
        grid=(b,),
        in_specs=[pl.BlockSpec((None, m, d), lambda i: (i, 0, 0)),
                  _resident((1, d)), _resident(wkv.shape)],
        out_specs=pl.BlockSpec((None, m, 2 * d), lambda i: (i, 0, 0)),
        out_shape=jax.ShapeDtypeStruct((b, m, 2 * d), BF16),
        compiler_params=pltpu.CompilerParams(
            dimension_semantics=("arbitrary",), vmem_limit_bytes=VMEM_LIMIT),
        name="mem_kv",
    )(mem, g, wkv)


def _xattn_kernel(x_ref, g_ref, wq_ref, k_ref, v_ref, wo_ref, o_ref, att_ref):
    x = x_ref[...]
    h = _rms(x, g_ref[...]).astype(BF16)
    q = jnp.dot(h, wq_ref[...], preferred_element_type=F32) * (X_HEAD_DIM ** -0.5)
    q = q.astype(BF16)

    def head_cols(hd):
        return slice(hd * X_HEAD_DIM, (hd + 1) * X_HEAD_DIM)

    def scores(hd):
        return lax.dot_general(k_ref[:, head_cols(hd)], q[:, head_cols(hd)],
                               (((1,), (1,)), ((), ())), preferred_element_type=F32)

    s_next = scores(0)
    for hd in range(X_HEADS):
        s = s_next
        if hd + 1 < X_HEADS:
            s_next = scores(hd + 1)
        m = jnp.max(s, axis=0, keepdims=True)
        ex = jnp.exp(s - m)
        pr = (ex * (1.0 / jnp.sum(ex, axis=0, keepdims=True))).astype(BF16)
        att_ref[:, head_cols(hd)] = lax.dot_general(
            pr, v_ref[:, head_cols(hd)], (((0,), (0,)), ((), ())),
            preferred_element_type=F32).astype(BF16)
    o_ref[...] = x + jnp.dot(att_ref[...], wo_ref[...], preferred_element_type=F32)


def _xattn(x, batch, seq, g, wq, kv, wo):
    n, d = x.shape
    tm = ROW_TILE
    spt = seq // tm
    m = kv.shape[1]
    row_map = lambda b, j: (b * spt + j, 0)
    return pl.pallas_call(
        _xattn_kernel,
        grid=(batch, spt),
        in_specs=[pl.BlockSpec((tm, d), row_map),
                  _resident((1, d)), _resident(wq.shape),
                  pl.BlockSpec((None, m, d), lambda b, j: (b, 0, 0)),
                  pl.BlockSpec((None, m, d), lambda b, j: (b, 0, 1)),
                  _resident(wo.shape)],
        out_specs=pl.BlockSpec((tm, d), row_map),
        out_shape=jax.ShapeDtypeStruct((n, d), F32),
        scratch_shapes=[pltpu.VMEM((tm, d), BF16)],
        compiler_params=pltpu.CompilerParams(
            dimension_semantics=("arbitrary", "arbitrary"), vmem_limit_bytes=VMEM_LIMIT),
        name="xattn",
    )(x, g, wq, kv, kv, wo)


def kernel(x, mem, positions, ffn1_norm, ffn1_w_gate, ffn1_w_up, ffn1_w_down, mix_norm, w_in,
           attn_sinks, pool_w, pool_scale, attn_out_norm, pool_out_norm, w_out, xattn_norm,
           mem_norm, xattn_wq, xattn_wkv, xattn_wo, ffn2_norm, ffn2_w_gate, ffn2_w_up,
           ffn2_w_down, final_norm):
    batch, seq, d = x.shape
    depth = w_in.shape[0]
    assert seq % ROW_TILE == 0 and ROW_TILE % BLOCK == 0 and d == D_MODEL
    n = batch * seq
    row = lambda v: v.reshape(1, -1)
    pool_w2 = pool_w.reshape(depth, POOL_WIDTH, POOL_GROUP_DIM)
    ffn1_w = (ffn1_w_gate, ffn1_w_up, ffn1_w_down)
    ffn2_w = (ffn2_w_gate, ffn2_w_up, ffn2_w_down)
    rest_w = (w_in, pool_w2, w_out, xattn_wq, xattn_wkv, xattn_wo) + ffn2_w

    xf = x.reshape(n, d)
    fg = row(final_norm)
    w1 = [w[0].astype(BF16) for w in ffn1_w]
    cos_t, sin_t = _rope_tables(positions)
    for l in range(depth):
        xf, wl = _ffn(xf, row(ffn1_norm[l]), *w1, fg, final=False,
                      casts=[(w, l) for w in rest_w])
        win_b, pw_b, wout_b, wq_b, wkv_b, wo_b = wl[:6]
        xf = _mixer(xf, cos_t, sin_t, batch, seq, attn_sinks[l], row(mix_norm[l]), win_b,
                    pw_b, row(pool_scale[l]), row(attn_out_norm[l]), row(pool_out_norm[l]), wout_b)
        kv = _mem_kv(mem, row(mem_norm[l]), wkv_b)
        xf = _xattn(xf, batch, seq, row(xattn_norm[l]), wq_b, kv, wo_b)
        last = l == depth - 1
        xf, w1 = _ffn(xf, row(ffn2_norm[l]), *wl[6:], fg, final=last,
                      casts=[] if last else [(w, l + 1) for w in ffn1_w])
    return xf.reshape(batch, seq, d)
```

```python
import functools

import jax
import jax.numpy as jnp
from jax import lax
from jax.experimental import pallas as pl
from jax.experimental.pallas import tpu as pltpu

F32 = jnp.float32
BF16 = jnp.bfloat16

D_MODEL = 1024
HEAD_DIM = 64
N_Q_HEADS = 8
N_KV_HEADS = 2
ATTN_WIDTH = N_Q_HEADS * HEAD_DIM
KV_WIDTH = N_KV_HEADS * HEAD_DIM
BLOCK = 128
POOL_WINDOWS = (2, 4, 8, 16)
POOL_GROUP_DIM = 128
POOL_WIDTH = 512
POOL_HALO = 16
POOL_TOP = 32
X_HEADS = 4
X_HEAD_DIM = 256
D_FF = 2816
FFN_RES = 0.5
EPS = 1e-6
ROPE_THETA = 10000.0
NEG = -1e30
LANES = 128
BF16_SUBLANES = 16

ROW_TILE = 1024
MXU_TILE = 256
FF_CHUNK = 6 * MXU_TILE
FF_CHUNKS = tuple((lo, min(lo + FF_CHUNK, D_FF)) for lo in range(0, D_FF, FF_CHUNK))
VMEM_LIMIT = 56 * 1024 * 1024


def _rms(x, g):
    ms = jnp.mean(x * x, axis=-1, keepdims=True)
    return x * lax.rsqrt(ms + EPS) * g


def _resident(shape):
    nd = len(shape)
    return pl.BlockSpec(shape, lambda *_: (0,) * nd, pipeline_mode=pl.Buffered(1))


def _rope_table_kernel(pos_ref, freq_ref, cos_ref, sin_ref):
    nfreq = HEAD_DIM // 2
    q = pos_ref.shape[0] // (LANES // nfreq)
    group = lax.broadcasted_iota(jnp.int32, (q, LANES), 1) // nfreq
    freq = freq_ref[...]
    ang = pos_ref[0:q, :].astype(F32) * freq
    for k in range(1, LANES // nfreq):
        ang = jnp.where(group == k, pos_ref[k * q:(k + 1) * q, :].astype(F32) * freq, ang)
    even = group % 2 == 0
    for t, ref, rotate_half_sign in ((jnp.cos(ang), cos_ref, False), (jnp.sin(ang), sin_ref, True)):
        pairs = (jnp.where(even, t, pltpu.roll(t, nfreq, 1)),
                 jnp.where(even, pltpu.roll(t, LANES - nfreq, 1), t))
        swaps = tuple(pltpu.roll(b, 2 * nfreq, 1) for b in pairs)
        for k in range(LANES // nfreq):
            b, swapped = pairs[k % 2], swaps[k % 2]
            full = jnp.where(group < 2, b, swapped) if k < 2 else jnp.where(group < 2, swapped, b)
            if rotate_half_sign:
                full = jnp.where(even, -full, full)
            ref[k * q:(k + 1) * q, :] = full


def _rope_tables(positions):
    n = positions.size
    inv_freq = ROPE_THETA ** (-jnp.arange(0, HEAD_DIM, 2, dtype=F32) / HEAD_DIM)
    freq = jnp.tile(inv_freq, LANES // (HEAD_DIM // 2)).reshape(1, LANES)
    tm = 2048
    return pl.pallas_call(
        _rope_table_kernel,
        grid=(n // tm,),
        in_specs=[pl.BlockSpec((tm, 1), lambda i: (i, 0)),
                  pl.BlockSpec((1, LANES), lambda i: (0, 0))],
        out_specs=[pl.BlockSpec((tm, LANES), lambda i: (i, 0))] * 2,
        out_shape=[jax.ShapeDtypeStruct((n, LANES), F32)] * 2,
        name="rope_tables",
    )(positions.reshape(n, 1), freq)


def _ffn_body(x, g, wg_ref, wu_ref, wd_ref):
    xn = _rms(x, g).astype(BF16)
    y = None
    for lo, hi in FF_CHUNKS:
        gate = jnp.dot(xn, wg_ref[:, lo:hi], preferred_element_type=F32)
        up = jnp.dot(xn, wu_ref[:, lo:hi], preferred_element_type=F32)
        h = (gate / (1.0 + jnp.exp(-gate)) * up).astype(BF16)
        part = jnp.dot(h, wd_ref[lo:hi, :], preferred_element_type=F32)
        y = part if y is None else y + part
    return x + FFN_RES * y


def _cast_blocks(rows, nsteps):
    rb = -(-rows // nsteps)
    rb = -(-rb // BF16_SUBLANES) * BF16_SUBLANES
    while rows % rb:
        rb += BF16_SUBLANES
    return rb, rows // rb


def _ffn_kernel(*refs, layer, final, n_casts):
    x_ref, g_ref, wg_ref, wu_ref, wd_ref, fg_ref = refs[:6]
    cast_in, (o_ref, *cast_out) = refs[6:6 + n_casts], refs[6 + n_casts:]
    out = _ffn_body(x_ref[...], g_ref[layer:layer + 1, :], wg_ref, wu_ref, wd_ref)
    if final:
        out = _rms(out, fg_ref[...])
    o_ref[...] = out
    for src, dst in zip(cast_in, cast_out):
        dst[...] = src[...].astype(BF16)


def _ffn(x, g, layer, wg, wu, wd, fg, final, casts=()):
    n, d = x.shape
    tm = ROW_TILE
    nsteps = n // tm
    in_specs = [pl.BlockSpec((tm, d), lambda i: (i, 0)),
                _resident(g.shape), _resident(wg.shape), _resident(wu.shape),
                _resident(wd.shape), _resident((1, d))]
    out_specs = [pl.BlockSpec((tm, d), lambda i: (i, 0))]
    out_shape = [jax.ShapeDtypeStruct((n, d), F32)]
    args = [x, g, wg, wu, wd, fg]
    for src, src_layer in casts:
        rows, cols = src.shape[1:]
        rb, nb = _cast_blocks(rows, nsteps)
        in_specs.append(pl.BlockSpec(
            (None, rb, cols), lambda i, sl=src_layer, nb=nb: (sl, jnp.minimum(i, nb - 1), 0)))
        out_specs.append(pl.BlockSpec((rb, cols), lambda i, nb=nb: (jnp.minimum(i, nb - 1), 0)))
        out_shape.append(jax.ShapeDtypeStruct((rows, cols), BF16))
        args.append(src)
    res = pl.pallas_call(
        functools.partial(_ffn_kernel, layer=layer, final=final, n_casts=len(casts)),
        grid=(nsteps,),
        in_specs=in_specs,
        out_specs=out_specs,
        out_shape=out_shape,
        compiler_params=pltpu.CompilerParams(
            dimension_semantics=("arbitrary",), vmem_limit_bytes=VMEM_LIMIT),
        name="ffn_final" if final else "ffn",
    )(*args)
    return res[0], res[1:]


def _rope(t, cos, sin_signed):
    lane = lax.broadcasted_iota(jnp.int32, t.shape, 1)
    first_half = (lane % HEAD_DIM) < (HEAD_DIM // 2)
    swapped = jnp.where(first_half,
                        pltpu.roll(t, LANES - HEAD_DIM // 2, 1),
                        pltpu.roll(t, HEAD_DIM // 2, 1))
    return t * cos + swapped * sin_signed


def _mixer_kernel(sinks_ref, x_ref, cos_ref, sin_ref, g_ref, win_ref, pw_ref, ps_ref,
                  an_ref, pn_ref, wout_ref, o_ref,
                  kd, vd, ub, l1, l2, l3, oa_ref, ob_ref, *, layer):
    j = pl.program_id(1)
    tm = x_ref.shape[0]
    nblk = tm // BLOCK
    group = N_Q_HEADS // N_KV_HEADS

    @pl.when(j == 0)
    def _():
        kd[:, 0:BLOCK, :] = jnp.zeros((N_KV_HEADS, BLOCK, LANES), BF16)
        vd[:, 0:BLOCK, :] = jnp.zeros((N_KV_HEADS, BLOCK, LANES), BF16)
        ub[0:POOL_TOP, :] = jnp.zeros((POOL_TOP, POOL_WIDTH), F32)

    x = x_ref[...]
    lrow = slice(layer, layer + 1)
    h = _rms(x, g_ref[lrow, :]).astype(BF16)
    proj = jnp.dot(h, win_ref[...], preferred_element_type=F32)
    cos = cos_ref[...]
    sin = sin_ref[...]

    low_t = lax.broadcasted_iota(jnp.int32, (tm, LANES), 1) < HEAD_DIM
    k = _rope(proj[:, ATTN_WIDTH:ATTN_WIDTH + KV_WIDTH], cos, sin)
    v = proj[:, ATTN_WIDTH + KV_WIDTH:ATTN_WIDTH + 2 * KV_WIDTH]
    kr = pltpu.roll(k, HEAD_DIM, 1)
    vr = pltpu.roll(v, HEAD_DIM, 1)
    kd[0, BLOCK:BLOCK + tm, :] = jnp.where(low_t, k, kr).astype(BF16)
    kd[1, BLOCK:BLOCK + tm, :] = jnp.where(low_t, kr, k).astype(BF16)
    vd[0, BLOCK:BLOCK + tm, :] = jnp.where(low_t, v, vr).astype(BF16)
    vd[1, BLOCK:BLOCK + tm, :] = jnp.where(low_t, vr, v).astype(BF16)
    ub[POOL_TOP:POOL_TOP + tm, :] = proj[:, ATTN_WIDTH + 2 * KV_WIDTH:]

    low = lax.broadcasted_iota(jnp.int32, (BLOCK, LANES), 1) < HEAD_DIM
    tri_t = (lax.broadcasted_iota(jnp.int32, (BLOCK, BLOCK), 0)
             <= lax.broadcasted_iota(jnp.int32, (BLOCK, BLOCK), 1))
    tri_tb = tri_t.astype(BF16)
    first = j == 0
    scale = HEAD_DIM ** -0.5
    zero_bf = jnp.zeros((BLOCK, LANES), BF16)

    q_pairs = [(_rope(proj[:, p * LANES:(p + 1) * LANES], cos, sin) * scale).astype(BF16)
               for p in range(ATTN_WIDTH // LANES)]
    def scores(kvh, b):
        rows = slice(b * BLOCK, (b + 1) * BLOCK)
        pieces = []
        for idx in range(group):
            qb = q_pairs[(kvh * group + idx) // 2][rows, :]
            pieces.append(jnp.where(low if idx % 2 == 0 else ~low, qb, zero_bf))
        q4 = jnp.concatenate(pieces, axis=0)
        return lax.dot_general(kd[kvh, b * BLOCK:(b + 2) * BLOCK, :], q4, (((1,), (1,)), ((), ())),
                               preferred_element_type=F32)

    def attend(kvh, b, s4):
        rows = slice(b * BLOCK, (b + 1) * BLOCK)
        probs = []
        for idx in range(group):
            hcols = slice(idx * BLOCK, (idx + 1) * BLOCK)
            s_prev = s4[0:BLOCK, hcols]
            if b == 0:
                s_prev = jnp.where(first, NEG, s_prev)
            s = jnp.where(tri_t, s4[BLOCK:2 * BLOCK, hcols], s_prev)
            sink = sinks_ref[layer, kvh * group + idx]
            m = jnp.maximum(jnp.max(s, axis=0, keepdims=True), sink)
            ex = jnp.exp(s - m)
            den = jnp.sum(ex, axis=0, keepdims=True) + jnp.exp(sink - m)
            pr = (ex * (1.0 / den)).astype(BF16)
            p_cur = pr * tri_tb
            probs.append(jnp.concatenate([pr - p_cur, p_cur], axis=0))
        p4 = jnp.concatenate(probs, axis=1)
        o4 = lax.dot_general(p4, vd[kvh, b * BLOCK:(b + 2) * BLOCK, :], (((0,), (0,)), ((), ())),
                             preferred_element_type=F32)
        for pp in range(group // 2):
            pcol = (kvh * (group // 2) + pp) * LANES
            oa_ref[rows, pcol:pcol + LANES] = jnp.where(
                low, o4[(2 * pp) * BLOCK:(2 * pp + 1) * BLOCK, :],
                o4[(2 * pp + 1) * BLOCK:(2 * pp + 2) * BLOCK, :])

    units = [(kvh, b) for kvh in range(N_KV_HEADS) for b in range(nblk)]
    s_next = scores(*units[0])
    for u, unit in enumerate(units):
        s_cur = s_next
        if u + 1 < len(units):
            s_next = scores(*units[u + 1])
        attend(*unit, s_cur)

    n1 = tm + POOL_TOP - 8
    l1[8:8 + n1, :] = ub[8:8 + n1, :] + ub[7:7 + n1, :]
    n2 = tm + POOL_TOP - 16
    c2 = POOL_GROUP_DIM
    l2[16:16 + n2, :] = l1[16:16 + n2, c2:] + l1[14:14 + n2, c2:]
    n3 = tm + POOL_TOP - 24
    l3[24:24 + n3, :] = l2[24:24 + n3, c2:] + l2[20:20 + n3, c2:]
    top = POOL_TOP
    sums = [l1[top:top + tm, 0:c2], l2[top:top + tm, 0:c2], l3[top:top + tm, 0:c2],
            l3[top:top + tm, c2:] + l3[top - 8:top - 8 + tm, c2:]]
    row = lax.broadcasted_iota(jnp.int32, (POOL_HALO, POOL_GROUP_DIM), 0) + j * tm
    for gi, w in enumerate(POOL_WINDOWS):
        cols = slice(gi * POOL_GROUP_DIM, (gi + 1) * POOL_GROUP_DIM)
        cnt = jnp.minimum(row + 1, w).astype(F32)
        mean = jnp.concatenate([sums[gi][0:POOL_HALO] / cnt, sums[gi][POOL_HALO:] * (1.0 / w)],
                               axis=0)
        pooled = (mean - ub[top:top + tm, cols]).astype(BF16)
        mixed = jnp.dot(pooled, pw_ref[cols, :], preferred_element_type=F32)
        ob_ref[:, cols] = mixed * ps_ref[lrow, cols]

    kd[:, 0:BLOCK, :] = kd[:, tm:tm + BLOCK, :]
    vd[:, 0:BLOCK, :] = vd[:, tm:tm + BLOCK, :]
    ub[POOL_TOP - POOL_HALO:POOL_TOP, :] = ub[tm + POOL_TOP - POOL_HALO:tm + POOL_TOP, :]

    na = _rms(oa_ref[...], an_ref[lrow, :]).astype(BF16)
    nb = _rms(ob_ref[...], pn_ref[lrow, :]).astype(BF16)
    y = jnp.dot(na, wout_ref[0:ATTN_WIDTH, :], preferred_element_type=F32)
    y = y + jnp.dot(nb, wout_ref[ATTN_WIDTH:, :], preferred_element_type=F32)
    o_ref[...] = x + y


def _mixer(x, cos_t, sin_t, batch, seq, layer, sinks, g, win, pw, ps, an, pn, wout):
    n, d = x.shape
    tm = ROW_TILE
    spt = seq // tm
    row_map = lambda b, j: (b * spt + j, 0)
    return pl.pallas_call(
        functools.partial(_mixer_kernel, layer=layer),
        grid=(batch, spt),
        in_specs=[pl.BlockSpec(memory_space=pltpu.SMEM),
                  pl.BlockSpec((tm, d), row_map),
                  pl.BlockSpec((tm, LANES), row_map),
                  pl.BlockSpec((tm, LANES), row_map),
                  _resident(g.shape), _resident(win.shape), _resident(pw.shape),
                  _resident(ps.shape), _resident(an.shape),
                  _resident(pn.shape), _resident(wout.shape)],
        out_specs=pl.BlockSpec((tm, d), row_map),
        out_shape=jax.ShapeDtypeStruct((n, d), F32),
        scratch_shapes=[pltpu.VMEM((N_KV_HEADS, BLOCK + tm, LANES), BF16),
                        pltpu.VMEM((N_KV_HEADS, BLOCK + tm, LANES), BF16),
                        pltpu.VMEM((POOL_TOP + tm, POOL_WIDTH), F32),
                        pltpu.VMEM((POOL_TOP + tm, POOL_WIDTH), F32),
                        pltpu.VMEM((POOL_TOP + tm, POOL_WIDTH - POOL_GROUP_DIM), F32),
                        pltpu.VMEM((POOL_TOP + tm, POOL_WIDTH - 2 * POOL_GROUP_DIM), F32),
                        pltpu.VMEM((tm, ATTN_WIDTH), F32),
                        pltpu.VMEM((tm, POOL_WIDTH), F32)],
        compiler_params=pltpu.CompilerParams(
            dimension_semantics=("arbitrary", "arbitrary"), vmem_limit_bytes=VMEM_LIMIT),
        name="mixer",
    )(sinks, x, cos_t, sin_t, g, win, pw, ps, an, pn, wout)


def _mem_kv_kernel(mem_ref, g_ref, wkv_ref, kv_ref, *, layer):
    mn = _rms(mem_ref[...], g_ref[layer:layer + 1, :]).astype(BF16)
    kv_ref[...] = jnp.dot(mn, wkv_ref[...], preferred_element_type=F32).astype(BF16)


def _mem_kv(mem, g, layer, wkv):
    b, m, d = mem.shape
    return pl.pallas_call(
        functools.partial(_mem_kv_kernel, layer=layer),
        grid=(b,),
        in_specs=[pl.BlockSpec((None, m, d), lambda i: (i, 0, 0)),
                  _resident(g.shape), _resident(wkv.shape)],
        out_specs=pl.BlockSpec((None, m, 2 * d), lambda i: (i, 0, 0)),
        out_shape=jax.ShapeDtypeStruct((b, m, 2 * d), BF16),
        compiler_params=pltpu.CompilerParams(
            dimension_semantics=("arbitrary",), vmem_limit_bytes=VMEM_LIMIT),
        name="mem_kv",
    )(mem, g, wkv)


def _xattn_kernel(x_ref, g_ref, wq_ref, k_ref, v_ref, wo_ref, o_ref, att_ref, *, layer):
    x = x_ref[...]
    h = _rms(x, g_ref[layer:layer + 1, :]).astype(BF16)
    q = jnp.dot(h, wq_ref[...], preferred_element_type=F32) * (X_HEAD_DIM ** -0.5)
    q = q.astype(BF16)

    def head_cols(hd):
        return slice(hd * X_HEAD_DIM, (hd + 1) * X_HEAD_DIM)

    def scores(hd):
        return lax.dot_general(k_ref[:, head_cols(hd)], q[:, head_cols(hd)],
                               (((1,), (1,)), ((), ())), preferred_element_type=F32)

    s_next = scores(0)
    for hd in range(X_HEADS):
        s = s_next
        if hd + 1 < X_HEADS:
            s_next = scores(hd + 1)
        m = jnp.max(s, axis=0, keepdims=True)
        ex = jnp.exp(s - m)
        pr = (ex * (1.0 / jnp.sum(ex, axis=0, keepdims=True))).astype(BF16)
        att_ref[:, head_cols(hd)] = lax.dot_general(
            pr, v_ref[:, head_cols(hd)], (((0,), (0,)), ((), ())),
            preferred_element_type=F32).astype(BF16)
    o_ref[...] = x + jnp.dot(att_ref[...], wo_ref[...], preferred_element_type=F32)


def _xattn(x, batch, seq, g, layer, wq, kv, wo):
    n, d = x.shape
    tm = ROW_TILE
    spt = seq // tm
    m = kv.shape[1]
    row_map = lambda b, j: (b * spt + j, 0)
    return pl.pallas_call(
        functools.partial(_xattn_kernel, layer=layer),
        grid=(batch, spt),
        in_specs=[pl.BlockSpec((tm, d), row_map),
                  _resident(g.shape), _resident(wq.shape),
                  pl.BlockSpec((None, m, d), lambda b, j: (b, 0, 0)),
                  pl.BlockSpec((None, m, d), lambda b, j: (b, 0, 1)),
                  _resident(wo.shape)],
        out_specs=pl.BlockSpec((tm, d), row_map),
        out_shape=jax.ShapeDtypeStruct((n, d), F32),
        scratch_shapes=[pltpu.VMEM((tm, d), BF16)],
        compiler_params=pltpu.CompilerParams(
            dimension_semantics=("arbitrary", "arbitrary"), vmem_limit_bytes=VMEM_LIMIT),
        name="xattn",
    )(x, g, wq, kv, kv, wo)


def kernel(x, mem, positions, ffn1_norm, ffn1_w_gate, ffn1_w_up, ffn1_w_down, mix_norm, w_in,
           attn_sinks, pool_w, pool_scale, attn_out_norm, pool_out_norm, w_out, xattn_norm,
           mem_norm, xattn_wq, xattn_wkv, xattn_wo, ffn2_norm, ffn2_w_gate, ffn2_w_up,
           ffn2_w_down, final_norm):
    batch, seq, d = x.shape
    depth = w_in.shape[0]
    assert seq % ROW_TILE == 0 and ROW_TILE % BLOCK == 0 and d == D_MODEL
    n = batch * seq
    pool_w2 = pool_w.reshape(depth, POOL_WIDTH, POOL_GROUP_DIM)
    ffn1_w = (ffn1_w_gate, ffn1_w_up, ffn1_w_down)
    ffn2_w = (ffn2_w_gate, ffn2_w_up, ffn2_w_down)
    rest_w = (w_in, pool_w2, w_out, xattn_wq, xattn_wkv, xattn_wo) + ffn2_w

    xf = x.reshape(n, d)
    fg = final_norm.reshape(1, d)
    w1 = [w[0].astype(BF16) for w in ffn1_w]
    cos_t, sin_t = _rope_tables(positions)
    for l in range(depth):
        xf, wl = _ffn(xf, ffn1_norm, l, *w1, fg, final=False, casts=[(w, l) for w in rest_w])
        win_b, pw_b, wout_b, wq_b, wkv_b, wo_b = wl[:6]
        xf = _mixer(xf, cos_t, sin_t, batch, seq, l, attn_sinks, mix_norm, win_b, pw_b,
                    pool_scale, attn_out_norm, pool_out_norm, wout_b)
        kv = _mem_kv(mem, mem_norm, l, wkv_b)
        xf = _xattn(xf, batch, seq, xattn_norm, l, wq_b, kv, wo_b)
        last = l == depth - 1
        xf, w1 = _ffn(xf, ffn2_norm, l, *wl[6:], fg, final=last,
                      casts=[] if last else [(w, l + 1) for w in ffn1_w])
    return xf.reshape(batch, seq, d)
```

```python
import functools

import jax
import jax.numpy as jnp
from jax import lax
from jax.experimental import pallas as pl
from jax.experimental.pallas import tpu as pltpu

F32 = jnp.float32
BF16 = jnp.bfloat16

D_MODEL = 1024
HEAD_DIM = 64
N_Q_HEADS = 8
N_KV_HEADS = 2
ATTN_WIDTH = N_Q_HEADS * HEAD_DIM
KV_WIDTH = N_KV_HEADS * HEAD_DIM
BLOCK = 128
POOL_WINDOWS = (2, 4, 8, 16)
POOL_GROUP_DIM = 128
POOL_WIDTH = 512
POOL_HALO = 16
POOL_TOP = 32
X_HEADS = 4
X_HEAD_DIM = 256
D_FF = 2816
FFN_RES = 0.5
EPS = 1e-6
ROPE_THETA = 10000.0
NEG = -1e30
LANES = 128
BF16_SUBLANES = 16

ROW_TILE = 1024
MXU_TILE = 256
FF_CHUNK = 6 * MXU_TILE
FF_CHUNKS = tuple((lo, min(lo + FF_CHUNK, D_FF)) for lo in range(0, D_FF, FF_CHUNK))
VMEM_LIMIT = 56 * 1024 * 1024


def _rms(x, g):
    ms = jnp.mean(x * x, axis=-1, keepdims=True)
    return x * lax.rsqrt(ms + EPS) * g


def _resident(shape):
    nd = len(shape)
    return pl.BlockSpec(shape, lambda *_: (0,) * nd, pipeline_mode=pl.Buffered(1))


def _rope_table_kernel(pos_ref, freq_ref, *refs, n_casts):
    cast_in, (cos_ref, sin_ref, *cast_out) = refs[:n_casts], refs[n_casts:]
    nfreq = HEAD_DIM // 2
    q = pos_ref.shape[0] // (LANES // nfreq)
    group = lax.broadcasted_iota(jnp.int32, (q, LANES), 1) // nfreq
    freq = freq_ref[...]
    ang = pos_ref[0:q, :].astype(F32) * freq
    for k in range(1, LANES // nfreq):
        ang = jnp.where(group == k, pos_ref[k * q:(k + 1) * q, :].astype(F32) * freq, ang)
    even = group % 2 == 0
    for t, ref, rotate_half_sign in ((jnp.cos(ang), cos_ref, False), (jnp.sin(ang), sin_ref, True)):
        pairs = (jnp.where(even, t, pltpu.roll(t, nfreq, 1)),
                 jnp.where(even, pltpu.roll(t, LANES - nfreq, 1), t))
        swaps = tuple(pltpu.roll(b, 2 * nfreq, 1) for b in pairs)
        for k in range(LANES // nfreq):
            b, swapped = pairs[k % 2], swaps[k % 2]
            full = jnp.where(group < 2, b, swapped) if k < 2 else jnp.where(group < 2, swapped, b)
            if rotate_half_sign:
                full = jnp.where(even, -full, full)
            ref[k * q:(k + 1) * q, :] = full
    _cast_job(cast_in, cast_out)


def _rope_tables(positions, casts=()):
    n = positions.size
    inv_freq = ROPE_THETA ** (-jnp.arange(0, HEAD_DIM, 2, dtype=F32) / HEAD_DIM)
    freq = jnp.tile(inv_freq, LANES // (HEAD_DIM // 2)).reshape(1, LANES)
    tm = 2048
    c_in, c_out, c_shape, c_args = _cast_specs(casts, n // tm)
    res = pl.pallas_call(
        functools.partial(_rope_table_kernel, n_casts=len(casts)),
        grid=(n // tm,),
        in_specs=[pl.BlockSpec((tm, 1), lambda i: (i, 0)),
                  pl.BlockSpec((1, LANES), lambda i: (0, 0))] + c_in,
        out_specs=[pl.BlockSpec((tm, LANES), lambda i: (i, 0))] * 2 + c_out,
        out_shape=[jax.ShapeDtypeStruct((n, LANES), F32)] * 2 + c_shape,
        compiler_params=pltpu.CompilerParams(dimension_semantics=("arbitrary",)),
        name="rope_tables",
    )(positions.reshape(n, 1), freq, *c_args)
    return res[:2], res[2:]


def _cast_blocks(rows, nsteps):
    rb = -(-rows // nsteps)
    rb = -(-rb // BF16_SUBLANES) * BF16_SUBLANES
    while rows % rb:
        rb += BF16_SUBLANES
    return rb, rows // rb


def _cast_specs(casts, nsteps):
    in_specs, out_specs, out_shape, args = [], [], [], []
    for src, src_layer in casts:
        rows, cols = src.shape[1:]
        rb, nb = _cast_blocks(rows, nsteps)
        in_specs.append(pl.BlockSpec(
            (None, rb, cols), lambda i, sl=src_layer, nb=nb: (sl, jnp.minimum(i, nb - 1), 0)))
        out_specs.append(pl.BlockSpec((rb, cols), lambda i, nb=nb: (jnp.minimum(i, nb - 1), 0)))
        out_shape.append(jax.ShapeDtypeStruct((rows, cols), BF16))
        args.append(src)
    return in_specs, out_specs, out_shape, args


def _cast_job(cast_in, cast_out):
    for src, dst in zip(cast_in, cast_out):
        dst[...] = src[...].astype(BF16)


def _ffn_body(x, g, wg_ref, wu_ref, wd_ref):
    xn = _rms(x, g).astype(BF16)
    y = None
    for lo, hi in FF_CHUNKS:
        gate = jnp.dot(xn, wg_ref[:, lo:hi], preferred_element_type=F32)
        up = jnp.dot(xn, wu_ref[:, lo:hi], preferred_element_type=F32)
        h = (gate / (1.0 + jnp.exp(-gate)) * up).astype(BF16)
        part = jnp.dot(h, wd_ref[lo:hi, :], preferred_element_type=F32)
        y = part if y is None else y + part
    return x + FFN_RES * y


def _ffn_kernel(*refs, layer, final, n_casts):
    x_ref, g_ref, wg_ref, wu_ref, wd_ref, fg_ref = refs[:6]
    cast_in, (o_ref, *cast_out) = refs[6:6 + n_casts], refs[6 + n_casts:]
    out = _ffn_body(x_ref[...], g_ref[layer:layer + 1, :], wg_ref, wu_ref, wd_ref)
    if final:
        out = _rms(out, fg_ref[...])
    o_ref[...] = out
    _cast_job(cast_in, cast_out)


def _ffn(x, g, layer, wg, wu, wd, fg, final, casts=()):
    n, d = x.shape
    tm = ROW_TILE
    nsteps = n // tm
    c_in, c_out, c_shape, c_args = _cast_specs(casts, nsteps)
    in_specs = [pl.BlockSpec((tm, d), lambda i: (i, 0)),
                _resident(g.shape), _resident(wg.shape), _resident(wu.shape),
                _resident(wd.shape), _resident((1, d))] + c_in
    out_specs = [pl.BlockSpec((tm, d), lambda i: (i, 0))] + c_out
    out_shape = [jax.ShapeDtypeStruct((n, d), F32)] + c_shape
    args = [x, g, wg, wu, wd, fg] + c_args
    res = pl.pallas_call(
        functools.partial(_ffn_kernel, layer=layer, final=final, n_casts=len(casts)),
        grid=(nsteps,),
        in_specs=in_specs,
        out_specs=out_specs,
        out_shape=out_shape,
        compiler_params=pltpu.CompilerParams(
            dimension_semantics=("arbitrary",), vmem_limit_bytes=VMEM_LIMIT),
        name="ffn_final" if final else "ffn",
    )(*args)
    return res[0], res[1:]


def _rope(t, cos, sin_signed):
    lane = lax.broadcasted_iota(jnp.int32, t.shape, 1)
    first_half = (lane % HEAD_DIM) < (HEAD_DIM // 2)
    swapped = jnp.where(first_half,
                        pltpu.roll(t, LANES - HEAD_DIM // 2, 1),
                        pltpu.roll(t, HEAD_DIM // 2, 1))
    return t * cos + swapped * sin_signed


def _mixer_body(x, j, layer, sinks_ref, cos_ref, sin_ref, g_ref, win_ref, pw_ref, ps_ref,
                an_ref, pn_ref, wout_ref, kd, vd, ub, l1, l2, l3, oa_ref, ob_ref):
    tm = x.shape[0]
    nblk = tm // BLOCK
    group = N_Q_HEADS // N_KV_HEADS

    @pl.when(j == 0)
    def _():
        kd[:, 0:BLOCK, :] = jnp.zeros((N_KV_HEADS, BLOCK, LANES), BF16)
        vd[:, 0:BLOCK, :] = jnp.zeros((N_KV_HEADS, BLOCK, LANES), BF16)
        ub[0:POOL_TOP, :] = jnp.zeros((POOL_TOP, POOL_WIDTH), F32)

    lrow = slice(layer, layer + 1)
    h = _rms(x, g_ref[lrow, :]).astype(BF16)
    proj = jnp.dot(h, win_ref[...], preferred_element_type=F32)
    cos = cos_ref[...]
    sin = sin_ref[...]

    low_t = lax.broadcasted_iota(jnp.int32, (tm, LANES), 1) < HEAD_DIM
    k = _rope(proj[:, ATTN_WIDTH:ATTN_WIDTH + KV_WIDTH], cos, sin)
    v = proj[:, ATTN_WIDTH + KV_WIDTH:ATTN_WIDTH + 2 * KV_WIDTH]
    kr = pltpu.roll(k, HEAD_DIM, 1)
    vr = pltpu.roll(v, HEAD_DIM, 1)
    kd[0, BLOCK:BLOCK + tm, :] = jnp.where(low_t, k, kr).astype(BF16)
    kd[1, BLOCK:BLOCK + tm, :] = jnp.where(low_t, kr, k).astype(BF16)
    vd[0, BLOCK:BLOCK + tm, :] = jnp.where(low_t, v, vr).astype(BF16)
    vd[1, BLOCK:BLOCK + tm, :] = jnp.where(low_t, vr, v).astype(BF16)
    ub[POOL_TOP:POOL_TOP + tm, :] = proj[:, ATTN_WIDTH + 2 * KV_WIDTH:]

    low = lax.broadcasted_iota(jnp.int32, (BLOCK, LANES), 1) < HEAD_DIM
    tri_t = (lax.broadcasted_iota(jnp.int32, (BLOCK, BLOCK), 0)
             <= lax.broadcasted_iota(jnp.int32, (BLOCK, BLOCK), 1))
    tri_tb = tri_t.astype(BF16)
    first = j == 0
    scale = HEAD_DIM ** -0.5
    zero_bf = jnp.zeros((BLOCK, LANES), BF16)

    q_pairs = [(_rope(proj[:, p * LANES:(p + 1) * LANES], cos, sin) * scale).astype(BF16)
               for p in range(ATTN_WIDTH // LANES)]
    def scores(kvh, b):
        rows = slice(b * BLOCK, (b + 1) * BLOCK)
        pieces = []
        for idx in range(group):
            qb = q_pairs[(kvh * group + idx) // 2][rows, :]
            pieces.append(jnp.where(low if idx % 2 == 0 else ~low, qb, zero_bf))
        q4 = jnp.concatenate(pieces, axis=0)
        return lax.dot_general(kd[kvh, b * BLOCK:(b + 2) * BLOCK, :], q4, (((1,), (1,)), ((), ())),
                               preferred_element_type=F32)

    def attend(kvh, b, s4):
        rows = slice(b * BLOCK, (b + 1) * BLOCK)
        probs = []
        for idx in range(group):
            hcols = slice(idx * BLOCK, (idx + 1) * BLOCK)
            s_prev = s4[0:BLOCK, hcols]
            if b == 0:
                s_prev = jnp.where(first, NEG, s_prev)
            s = jnp.where(tri_t, s4[BLOCK:2 * BLOCK, hcols], s_prev)
            sink = sinks_ref[layer, kvh * group + idx]
            m = jnp.maximum(jnp.max(s, axis=0, keepdims=True), sink)
            ex = jnp.exp(s - m)
            den = jnp.sum(ex, axis=0, keepdims=True) + jnp.exp(sink - m)
            pr = (ex * (1.0 / den)).astype(BF16)
            p_cur = pr * tri_tb
            probs.append(jnp.concatenate([pr - p_cur, p_cur], axis=0))
        p4 = jnp.concatenate(probs, axis=1)
        o4 = lax.dot_general(p4, vd[kvh, b * BLOCK:(b + 2) * BLOCK, :], (((0,), (0,)), ((), ())),
                             preferred_element_type=F32)
        for pp in range(group // 2):
            pcol = (kvh * (group // 2) + pp) * LANES
            oa_ref[rows, pcol:pcol + LANES] = jnp.where(
                low, o4[(2 * pp) * BLOCK:(2 * pp + 1) * BLOCK, :],
                o4[(2 * pp + 1) * BLOCK:(2 * pp + 2) * BLOCK, :])

    units = [(kvh, b) for kvh in range(N_KV_HEADS) for b in range(nblk)]
    s_next = scores(*units[0])
    for u, unit in enumerate(units):
        s_cur = s_next
        if u + 1 < len(units):
            s_next = scores(*units[u + 1])
        attend(*unit, s_cur)

    n1 = tm + POOL_TOP - 8
    l1[8:8 + n1, :] = ub[8:8 + n1, :] + ub[7:7 + n1, :]
    n2 = tm + POOL_TOP - 16
    c2 = POOL_GROUP_DIM
    l2[16:16 + n2, :] = l1[16:16 + n2, c2:] + l1[14:14 + n2, c2:]
    n3 = tm + POOL_TOP - 24
    l3[24:24 + n3, :] = l2[24:24 + n3, c2:] + l2[20:20 + n3, c2:]
    top = POOL_TOP
    sums = [l1[top:top + tm, 0:c2], l2[top:top + tm, 0:c2], l3[top:top + tm, 0:c2],
            l3[top:top + tm, c2:] + l3[top - 8:top - 8 + tm, c2:]]
    row = lax.broadcasted_iota(jnp.int32, (POOL_HALO, POOL_GROUP_DIM), 0) + j * tm
    for gi, w in enumerate(POOL_WINDOWS):
        cols = slice(gi * POOL_GROUP_DIM, (gi + 1) * POOL_GROUP_DIM)
        cnt = jnp.minimum(row + 1, w).astype(F32)
        mean = jnp.concatenate([sums[gi][0:POOL_HALO] / cnt, sums[gi][POOL_HALO:] * (1.0 / w)],
                               axis=0)
        pooled = (mean - ub[top:top + tm, cols]).astype(BF16)
        mixed = jnp.dot(pooled, pw_ref[cols, :], preferred_element_type=F32)
        ob_ref[:, cols] = mixed * ps_ref[lrow, cols]

    kd[:, 0:BLOCK, :] = kd[:, tm:tm + BLOCK, :]
    vd[:, 0:BLOCK, :] = vd[:, tm:tm + BLOCK, :]
    ub[POOL_TOP - POOL_HALO:POOL_TOP, :] = ub[tm + POOL_TOP - POOL_HALO:tm + POOL_TOP, :]

    na = _rms(oa_ref[...], an_ref[lrow, :]).astype(BF16)
    nb = _rms(ob_ref[...], pn_ref[lrow, :]).astype(BF16)
    y = jnp.dot(na, wout_ref[0:ATTN_WIDTH, :], preferred_element_type=F32)
    y = y + jnp.dot(nb, wout_ref[ATTN_WIDTH:, :], preferred_element_type=F32)
    return x + y


def _mem_kv_kernel(mem_ref, g_ref, wkv_ref, kv_ref, *, layer):
    mn = _rms(mem_ref[...], g_ref[layer:layer + 1, :]).astype(BF16)
    kv_ref[...] = jnp.dot(mn, wkv_ref[...], preferred_element_type=F32).astype(BF16)


def _mem_kv(mem, g, layer, wkv):
    b, m, d = mem.shape
    return pl.pallas_call(
        functools.partial(_mem_kv_kernel, layer=layer),
        grid=(b,),
        in_specs=[pl.BlockSpec((None, m, d), lambda i: (i, 0, 0)),
                  _resident(g.shape), _resident(wkv.shape)],
        out_specs=pl.BlockSpec((None, m, 2 * d), lambda i: (i, 0, 0)),
        out_shape=jax.ShapeDtypeStruct((b, m, 2 * d), BF16),
        compiler_params=pltpu.CompilerParams(
            dimension_semantics=("arbitrary",), vmem_limit_bytes=VMEM_LIMIT),
        name="mem_kv",
    )(mem, g, wkv)


def _xattn_body(x, g, wq_ref, k_ref, v_ref, wo_ref, att_ref):
    h = _rms(x, g).astype(BF16)
    q = jnp.dot(h, wq_ref[...], preferred_element_type=F32) * (X_HEAD_DIM ** -0.5)
    q = q.astype(BF16)

    def head_cols(hd):
        return slice(hd * X_HEAD_DIM, (hd + 1) * X_HEAD_DIM)

    def scores(hd):
        return lax.dot_general(k_ref[:, head_cols(hd)], q[:, head_cols(hd)],
                               (((1,), (1,)), ((), ())), preferred_element_type=F32)

    s_next = scores(0)
    for hd in range(X_HEADS):
        s = s_next
        if hd + 1 < X_HEADS:
            s_next = scores(hd + 1)
        m = jnp.max(s, axis=0, keepdims=True)
        ex = jnp.exp(s - m)
        pr = (ex * (1.0 / jnp.sum(ex, axis=0, keepdims=True))).astype(BF16)
        att_ref[:, head_cols(hd)] = lax.dot_general(
            pr, v_ref[:, head_cols(hd)], (((0,), (0,)), ((), ())),
            preferred_element_type=F32).astype(BF16)
    return x + jnp.dot(att_ref[...], wo_ref[...], preferred_element_type=F32)


N_MIXER_IN = 9


def _mix_kernel(sinks_ref, x_ref, *refs, layer):
    mixer_in, (xg_ref, wq_ref, k_ref, v_ref, wo_ref, o_ref, *scratch) = (
        refs[:N_MIXER_IN], refs[N_MIXER_IN:])
    *mixer_scratch, att_ref = scratch
    x = _mixer_body(x_ref[...], pl.program_id(1), layer, sinks_ref, *mixer_in, *mixer_scratch)
    o_ref[...] = _xattn_body(x, xg_ref[layer:layer + 1, :], wq_ref, k_ref, v_ref, wo_ref, att_ref)


def _mix(x, cos_t, sin_t, batch, seq, layer, sinks, g, win, pw, ps, an, pn, wout, xg, wq, kv, wo):
    n, d = x.shape
    tm = ROW_TILE
    spt = seq // tm
    m = kv.shape[1]
    row_map = lambda b, j: (b * spt + j, 0)
    mixer_in = (cos_t, sin_t, g, win, pw, ps, an, pn, wout)
    assert len(mixer_in) == N_MIXER_IN
    return pl.pallas_call(
        functools.partial(_mix_kernel, layer=layer),
        grid=(batch, spt),
        in_specs=[pl.BlockSpec(memory_space=pltpu.SMEM),
                  pl.BlockSpec((tm, d), row_map),
                  pl.BlockSpec((tm, LANES), row_map),
                  pl.BlockSpec((tm, LANES), row_map),
                  _resident(g.shape), _resident(win.shape), _resident(pw.shape),
                  _resident(ps.shape), _resident(an.shape),
                  _resident(pn.shape), _resident(wout.shape),
                  _resident(xg.shape), _resident(wq.shape),
                  pl.BlockSpec((None, m, d), lambda b, j: (b, 0, 0)),
                  pl.BlockSpec((None, m, d), lambda b, j: (b, 0, 1)),
                  _resident(wo.shape)],
        out_specs=pl.BlockSpec((tm, d), row_map),
        out_shape=jax.ShapeDtypeStruct((n, d), F32),
        scratch_shapes=[pltpu.VMEM((N_KV_HEADS, BLOCK + tm, LANES), BF16),
                        pltpu.VMEM((N_KV_HEADS, BLOCK + tm, LANES), BF16),
                        pltpu.VMEM((POOL_TOP + tm, POOL_WIDTH), F32),
                        pltpu.VMEM((POOL_TOP + tm, POOL_WIDTH), F32),
                        pltpu.VMEM((POOL_TOP + tm, POOL_WIDTH - POOL_GROUP_DIM), F32),
                        pltpu.VMEM((POOL_TOP + tm, POOL_WIDTH - 2 * POOL_GROUP_DIM), F32),
                        pltpu.VMEM((tm, ATTN_WIDTH), F32),
                        pltpu.VMEM((tm, POOL_WIDTH), F32),
                        pltpu.VMEM((tm, d), BF16)],
        compiler_params=pltpu.CompilerParams(
            dimension_semantics=("arbitrary", "arbitrary"), vmem_limit_bytes=VMEM_LIMIT),
        name="mix",
    )(sinks, x, *mixer_in, xg, wq, kv, kv, wo)


def kernel(x, mem, positions, ffn1_norm, ffn1_w_gate, ffn1_w_up, ffn1_w_down, mix_norm, w_in,
           attn_sinks, pool_w, pool_scale, attn_out_norm, pool_out_norm, w_out, xattn_norm,
           mem_norm, xattn_wq, xattn_wkv, xattn_wo, ffn2_norm, ffn2_w_gate, ffn2_w_up,
           ffn2_w_down, final_norm):
    batch, seq, d = x.shape
    depth = w_in.shape[0]
    assert seq % ROW_TILE == 0 and ROW_TILE % BLOCK == 0 and d == D_MODEL
    n = batch * seq
    pool_w2 = pool_w.reshape(depth, POOL_WIDTH, POOL_GROUP_DIM)
    ffn1_w = (ffn1_w_gate, ffn1_w_up, ffn1_w_down)
    ffn2_w = (ffn2_w_gate, ffn2_w_up, ffn2_w_down)
    rest_w = (w_in, pool_w2, w_out, xattn_wq, xattn_wkv, xattn_wo) + ffn2_w

    xf = x.reshape(n, d)
    fg = final_norm.reshape(1, d)
    (cos_t, sin_t), w1 = _rope_tables(positions, casts=[(w, 0) for w in ffn1_w])
    for l in range(depth):
        xf, wl = _ffn(xf, ffn1_norm, l, *w1, fg, final=False, casts=[(w, l) for w in rest_w])
        win_b, pw_b, wout_b, wq_b, wkv_b, wo_b = wl[:6]
        kv = _mem_kv(mem, mem_norm, l, wkv_b)
        xf = _mix(xf, cos_t, sin_t, batch, seq, l, attn_sinks, mix_norm, win_b, pw_b,
                  pool_scale, attn_out_norm, pool_out_norm, wout_b, xattn_norm, wq_b, kv, wo_b)
        last = l == depth - 1
        xf, w1 = _ffn(xf, ffn2_norm, l, *wl[6:], fg, final=last,
                      casts=[] if last else [(w, l + 1) for w in ffn1_w])
    return xf.reshape(batch, seq, d)
```

```python
import functools

import jax
import jax.numpy as jnp
from jax import lax
from jax.experimental import pallas as pl
from jax.experimental.pallas import tpu as pltpu

F32 = jnp.float32
BF16 = jnp.bfloat16

D_MODEL = 1024
HEAD_DIM = 64
N_Q_HEADS = 8
N_KV_HEADS = 2
ATTN_WIDTH = N_Q_HEADS * HEAD_DIM
KV_WIDTH = N_KV_HEADS * HEAD_DIM
BLOCK = 128
POOL_WINDOWS = (2, 4, 8, 16)
POOL_GROUP_DIM = 128
POOL_WIDTH = 512
POOL_HALO = 16
POOL_TOP = 32
X_HEADS = 4
X_HEAD_DIM = 256
D_FF = 2816
FFN_RES = 0.5
EPS = 1e-6
ROPE_THETA = 10000.0
NEG = -1e30
LOG2_E = 1.4426950408889634
LANES = 128
BF16_SUBLANES = 16

ROW_TILE = 1024
MXU_TILE = 256
FF_CHUNK = 6 * MXU_TILE
FF_CHUNKS = tuple((lo, min(lo + FF_CHUNK, D_FF)) for lo in range(0, D_FF, FF_CHUNK))
VMEM_LIMIT = 56 * 1024 * 1024


def _rms(x, g):
    ms = jnp.mean(x * x, axis=-1, keepdims=True)
    return x * lax.rsqrt(ms + EPS) * g


def _resident(shape):
    nd = len(shape)
    return pl.BlockSpec(shape, lambda *_: (0,) * nd, pipeline_mode=pl.Buffered(1))


def _rope_table_kernel(pos_ref, freq_ref, *refs, n_casts):
    cast_in, (cos_ref, sin_ref, *cast_out) = refs[:n_casts], refs[n_casts:]
    nfreq = HEAD_DIM // 2
    q = pos_ref.shape[0] // (LANES // nfreq)
    group = lax.broadcasted_iota(jnp.int32, (q, LANES), 1) // nfreq
    freq = freq_ref[...]
    ang = pos_ref[0:q, :].astype(F32) * freq
    for k in range(1, LANES // nfreq):
        ang = jnp.where(group == k, pos_ref[k * q:(k + 1) * q, :].astype(F32) * freq, ang)
    even = group % 2 == 0
    for t, ref, rotate_half_sign in ((jnp.cos(ang), cos_ref, False), (jnp.sin(ang), sin_ref, True)):
        pairs = (jnp.where(even, t, pltpu.roll(t, nfreq, 1)),
                 jnp.where(even, pltpu.roll(t, LANES - nfreq, 1), t))
        swaps = tuple(pltpu.roll(b, 2 * nfreq, 1) for b in pairs)
        for k in range(LANES // nfreq):
            b, swapped = pairs[k % 2], swaps[k % 2]
            full = jnp.where(group < 2, b, swapped) if k < 2 else jnp.where(group < 2, swapped, b)
            if rotate_half_sign:
                full = jnp.where(even, -full, full)
            ref[k * q:(k + 1) * q, :] = full
    _cast_job(cast_in, cast_out)


def _rope_tables(positions, casts=()):
    n = positions.size
    inv_freq = ROPE_THETA ** (-jnp.arange(0, HEAD_DIM, 2, dtype=F32) / HEAD_DIM)
    freq = jnp.tile(inv_freq, LANES // (HEAD_DIM // 2)).reshape(1, LANES)
    tm = 2048
    c_in, c_out, c_shape, c_args = _cast_specs(casts, n // tm)
    res = pl.pallas_call(
        functools.partial(_rope_table_kernel, n_casts=len(casts)),
        grid=(n // tm,),
        in_specs=[pl.BlockSpec((tm, 1), lambda i: (i, 0)),
                  pl.BlockSpec((1, LANES), lambda i: (0, 0))] + c_in,
        out_specs=[pl.BlockSpec((tm, LANES), lambda i: (i, 0))] * 2 + c_out,
        out_shape=[jax.ShapeDtypeStruct((n, LANES), F32)] * 2 + c_shape,
        compiler_params=pltpu.CompilerParams(dimension_semantics=("arbitrary",)),
        name="rope_tables",
    )(positions.reshape(n, 1), freq, *c_args)
    return res[:2], res[2:]


def _cast_blocks(rows, nsteps):
    rb = -(-rows // nsteps)
    rb = -(-rb // BF16_SUBLANES) * BF16_SUBLANES
    while rows % rb:
        rb += BF16_SUBLANES
    return rb, rows // rb


def _cast_specs(casts, nsteps):
    in_specs, out_specs, out_shape, args = [], [], [], []
    for src, src_layer in casts:
        rows, cols = src.shape[1:]
        rb, nb = _cast_blocks(rows, nsteps)
        in_specs.append(pl.BlockSpec(
            (None, rb, cols), lambda i, sl=src_layer, nb=nb: (sl, jnp.minimum(i, nb - 1), 0)))
        out_specs.append(pl.BlockSpec((rb, cols), lambda i, nb=nb: (jnp.minimum(i, nb - 1), 0)))
        out_shape.append(jax.ShapeDtypeStruct((rows, cols), BF16))
        args.append(src)
    return in_specs, out_specs, out_shape, args


def _cast_job(cast_in, cast_out):
    for src, dst in zip(cast_in, cast_out):
        dst[...] = src[...].astype(BF16)


def _ffn_body(x, g, wg_ref, wu_ref, wd_ref):
    xn = _rms(x, g).astype(BF16)
    y = None
    for lo, hi in FF_CHUNKS:
        gate = jnp.dot(xn, wg_ref[:, lo:hi], preferred_element_type=F32)
        up = jnp.dot(xn, wu_ref[:, lo:hi], preferred_element_type=F32)
        h = (gate / (1.0 + jnp.exp(-gate)) * up).astype(BF16)
        part = jnp.dot(h, wd_ref[lo:hi, :], preferred_element_type=F32)
        y = part if y is None else y + part
    return x + FFN_RES * y


def _ffn_kernel(*refs, layer, final, n_casts):
    x_ref, g_ref, wg_ref, wu_ref, wd_ref, fg_ref = refs[:6]
    cast_in, (o_ref, *cast_out) = refs[6:6 + n_casts], refs[6 + n_casts:]
    out = _ffn_body(x_ref[...], g_ref[layer:layer + 1, :], wg_ref, wu_ref, wd_ref)
    if final:
        out = _rms(out, fg_ref[...])
    o_ref[...] = out
    _cast_job(cast_in, cast_out)


def _ffn(x, g, layer, wg, wu, wd, fg, final, casts=()):
    n, d = x.shape
    tm = ROW_TILE
    nsteps = n // tm
    c_in, c_out, c_shape, c_args = _cast_specs(casts, nsteps)
    in_specs = [pl.BlockSpec((tm, d), lambda i: (i, 0)),
                _resident(g.shape), _resident(wg.shape), _resident(wu.shape),
                _resident(wd.shape), _resident((1, d))] + c_in
    out_specs = [pl.BlockSpec((tm, d), lambda i: (i, 0))] + c_out
    out_shape = [jax.ShapeDtypeStruct((n, d), F32)] + c_shape
    args = [x, g, wg, wu, wd, fg] + c_args
    res = pl.pallas_call(
        functools.partial(_ffn_kernel, layer=layer, final=final, n_casts=len(casts)),
        grid=(nsteps,),
        in_specs=in_specs,
        out_specs=out_specs,
        out_shape=out_shape,
        compiler_params=pltpu.CompilerParams(
            dimension_semantics=("arbitrary",), vmem_limit_bytes=VMEM_LIMIT),
        name="ffn_final" if final else "ffn",
    )(*args)
    return res[0], res[1:]


def _rope(t, cos, sin_signed):
    lane = lax.broadcasted_iota(jnp.int32, t.shape, 1)
    first_half = (lane % HEAD_DIM) < (HEAD_DIM // 2)
    swapped = jnp.where(first_half,
                        pltpu.roll(t, LANES - HEAD_DIM // 2, 1),
                        pltpu.roll(t, HEAD_DIM // 2, 1))
    return t * cos + swapped * sin_signed


def _mixer_body(x, j, layer, sinks_ref, cos_ref, sin_ref, g_ref, win_ref, pw_ref, ps_ref,
                an_ref, pn_ref, wout_ref, kd, vd, ub, l1, l2, l3, oa_ref, ob_ref):
    tm = x.shape[0]
    nblk = tm // BLOCK
    group = N_Q_HEADS // N_KV_HEADS

    @pl.when(j == 0)
    def _():
        kd[:, 0:BLOCK, :] = jnp.zeros((N_KV_HEADS, BLOCK, LANES), BF16)
        vd[:, 0:BLOCK, :] = jnp.zeros((N_KV_HEADS, BLOCK, LANES), BF16)
        ub[0:POOL_TOP, :] = jnp.zeros((POOL_TOP, POOL_WIDTH), F32)

    lrow = slice(layer, layer + 1)
    h = _rms(x, g_ref[lrow, :]).astype(BF16)
    proj = jnp.dot(h, win_ref[...], preferred_element_type=F32)
    cos = cos_ref[...]
    sin = sin_ref[...]

    low_t = lax.broadcasted_iota(jnp.int32, (tm, LANES), 1) < HEAD_DIM
    k = _rope(proj[:, ATTN_WIDTH:ATTN_WIDTH + KV_WIDTH], cos, sin)
    v = proj[:, ATTN_WIDTH + KV_WIDTH:ATTN_WIDTH + 2 * KV_WIDTH]
    kr = pltpu.roll(k, HEAD_DIM, 1)
    vr = pltpu.roll(v, HEAD_DIM, 1)
    kd[0, BLOCK:BLOCK + tm, :] = jnp.where(low_t, k, kr).astype(BF16)
    kd[1, BLOCK:BLOCK + tm, :] = jnp.where(low_t, kr, k).astype(BF16)
    vd[0, BLOCK:BLOCK + tm, :] = jnp.where(low_t, v, vr).astype(BF16)
    vd[1, BLOCK:BLOCK + tm, :] = jnp.where(low_t, vr, v).astype(BF16)
    ub[POOL_TOP:POOL_TOP + tm, :] = proj[:, ATTN_WIDTH + 2 * KV_WIDTH:]

    low = lax.broadcasted_iota(jnp.int32, (BLOCK, LANES), 1) < HEAD_DIM
    tri_t = (lax.broadcasted_iota(jnp.int32, (BLOCK, BLOCK), 0)
             <= lax.broadcasted_iota(jnp.int32, (BLOCK, BLOCK), 1))
    tri_tb = tri_t.astype(BF16)
    first = j == 0
    scale = HEAD_DIM ** -0.5 * LOG2_E
    zero_bf = jnp.zeros((BLOCK, LANES), BF16)

    q_pairs = [(_rope(proj[:, p * LANES:(p + 1) * LANES], cos, sin) * scale).astype(BF16)
               for p in range(ATTN_WIDTH // LANES)]
    def scores(kvh, b):
        rows = slice(b * BLOCK, (b + 1) * BLOCK)
        pieces = []
        for idx in range(group):
            qb = q_pairs[(kvh * group + idx) // 2][rows, :]
            pieces.append(jnp.where(low if idx % 2 == 0 else ~low, qb, zero_bf))
        q4 = jnp.concatenate(pieces, axis=0)
        return lax.dot_general(kd[kvh, b * BLOCK:(b + 2) * BLOCK, :], q4, (((1,), (1,)), ((), ())),
                               preferred_element_type=F32)

    def attend(kvh, b, s4):
        rows = slice(b * BLOCK, (b + 1) * BLOCK)
        probs = []
        for idx in range(group):
            hcols = slice(idx * BLOCK, (idx + 1) * BLOCK)
            s_prev = s4[0:BLOCK, hcols]
            if b == 0:
                s_prev = jnp.where(first, NEG, s_prev)
            s = jnp.where(tri_t, s4[BLOCK:2 * BLOCK, hcols], s_prev)
            sink = sinks_ref[layer, kvh * group + idx] * LOG2_E
            m = jnp.maximum(jnp.max(s, axis=0, keepdims=True), sink)
            ex = jnp.exp2(s - m)
            den = jnp.sum(ex, axis=0, keepdims=True) + jnp.exp2(sink - m)
            pr = (ex * (1.0 / den)).astype(BF16)
            p_cur = pr * tri_tb
            probs.append(jnp.concatenate([pr - p_cur, p_cur], axis=0))
        p4 = jnp.concatenate(probs, axis=1)
        o4 = lax.dot_general(p4, vd[kvh, b * BLOCK:(b + 2) * BLOCK, :], (((0,), (0,)), ((), ())),
                             preferred_element_type=F32)
        for pp in range(group // 2):
            pcol = (kvh * (group // 2) + pp) * LANES
            oa_ref[rows, pcol:pcol + LANES] = jnp.where(
                low, o4[(2 * pp) * BLOCK:(2 * pp + 1) * BLOCK, :],
                o4[(2 * pp + 1) * BLOCK:(2 * pp + 2) * BLOCK, :])

    units = [(kvh, b) for kvh in range(N_KV_HEADS) for b in range(nblk)]
    s_next = scores(*units[0])
    for u, unit in enumerate(units):
        s_cur = s_next
        if u + 1 < len(units):
            s_next = scores(*units[u + 1])
        attend(*unit, s_cur)

    n1 = tm + POOL_TOP - 8
    l1[8:8 + n1, :] = ub[8:8 + n1, :] + ub[7:7 + n1, :]
    n2 = tm + POOL_TOP - 16
    c2 = POOL_GROUP_DIM
    l2[16:16 + n2, :] = l1[16:16 + n2, c2:] + l1[14:14 + n2, c2:]
    n3 = tm + POOL_TOP - 24
    l3[24:24 + n3, :] = l2[24:24 + n3, c2:] + l2[20:20 + n3, c2:]
    top = POOL_TOP
    sums = [l1[top:top + tm, 0:c2], l2[top:top + tm, 0:c2], l3[top:top + tm, 0:c2],
            l3[top:top + tm, c2:] + l3[top - 8:top - 8 + tm, c2:]]
    row = lax.broadcasted_iota(jnp.int32, (POOL_HALO, POOL_GROUP_DIM), 0) + j * tm
    pooled = []
    for gi, w in enumerate(POOL_WINDOWS):
        cols = slice(gi * POOL_GROUP_DIM, (gi + 1) * POOL_GROUP_DIM)
        cnt = jnp.minimum(row + 1, w).astype(F32)
        mean = jnp.concatenate([sums[gi][0:POOL_HALO] / cnt, sums[gi][POOL_HALO:] * (1.0 / w)],
                               axis=0)
        pooled.append((mean - ub[top:top + tm, cols]).astype(BF16))
    zero_w = jnp.zeros((POOL_GROUP_DIM, POOL_GROUP_DIM), BF16)
    for gp in range(0, len(POOL_WINDOWS), 2):
        ca = slice(gp * POOL_GROUP_DIM, (gp + 1) * POOL_GROUP_DIM)
        cb = slice((gp + 1) * POOL_GROUP_DIM, (gp + 2) * POOL_GROUP_DIM)
        w_pair = jnp.concatenate(
            [jnp.concatenate([pw_ref[ca, :], zero_w], axis=1),
             jnp.concatenate([zero_w, pw_ref[cb, :]], axis=1)], axis=0)
        mixed = jnp.dot(jnp.concatenate(pooled[gp:gp + 2], axis=1), w_pair,
                        preferred_element_type=F32)
        cab = slice(gp * POOL_GROUP_DIM, (gp + 2) * POOL_GROUP_DIM)
        ob_ref[:, cab] = mixed * ps_ref[lrow, cab]

    kd[:, 0:BLOCK, :] = kd[:, tm:tm + BLOCK, :]
    vd[:, 0:BLOCK, :] = vd[:, tm:tm + BLOCK, :]
    ub[POOL_TOP - POOL_HALO:POOL_TOP, :] = ub[tm + POOL_TOP - POOL_HALO:tm + POOL_TOP, :]

    na = _rms(oa_ref[...], an_ref[lrow, :]).astype(BF16)
    nb = _rms(ob_ref[...], pn_ref[lrow, :]).astype(BF16)
    y = jnp.dot(na, wout_ref[0:ATTN_WIDTH, :], preferred_element_type=F32)
    y = y + jnp.dot(nb, wout_ref[ATTN_WIDTH:, :], preferred_element_type=F32)
    return x + y


def _mem_kv_kernel(mem_ref, g_ref, wkv_ref, kv_ref, *, layer):
    mn = _rms(mem_ref[...], g_ref[layer:layer + 1, :]).astype(BF16)
    kv_ref[...] = jnp.dot(mn, wkv_ref[...], preferred_element_type=F32).astype(BF16)


def _mem_kv(mem, g, layer, wkv):
    b, m, d = mem.shape
    return pl.pallas_call(
        functools.partial(_mem_kv_kernel, layer=layer),
        grid=(b,),
        in_specs=[pl.BlockSpec((None, m, d), lambda i: (i, 0, 0)),
                  _resident(g.shape), _resident(wkv.shape)],
        out_specs=pl.BlockSpec((None, m, 2 * d), lambda i: (i, 0, 0)),
        out_shape=jax.ShapeDtypeStruct((b, m, 2 * d), BF16),
        compiler_params=pltpu.CompilerParams(
            dimension_semantics=("arbitrary",), vmem_limit_bytes=VMEM_LIMIT),
        name="mem_kv",
    )(mem, g, wkv)


def _xattn_body(x, g, wq_ref, k_ref, v_ref, wo_ref, att_ref):
    h = _rms(x, g).astype(BF16)
    q = jnp.dot(h, wq_ref[...], preferred_element_type=F32) * (X_HEAD_DIM ** -0.5 * LOG2_E)
    q = q.astype(BF16)

    def head_cols(hd):
        return slice(hd * X_HEAD_DIM, (hd + 1) * X_HEAD_DIM)

    def scores(hd):
        return lax.dot_general(k_ref[:, head_cols(hd)], q[:, head_cols(hd)],
                               (((1,), (1,)), ((), ())), preferred_element_type=F32)

    s_next = scores(0)
    for hd in range(X_HEADS):
        s = s_next
        if hd + 1 < X_HEADS:
            s_next = scores(hd + 1)
        m = jnp.max(s, axis=0, keepdims=True)
        ex = jnp.exp2(s - m)
        pr = (ex * (1.0 / jnp.sum(ex, axis=0, keepdims=True))).astype(BF16)
        att_ref[:, head_cols(hd)] = lax.dot_general(
            pr, v_ref[:, head_cols(hd)], (((0,), (0,)), ((), ())),
            preferred_element_type=F32).astype(BF16)
    return x + jnp.dot(att_ref[...], wo_ref[...], preferred_element_type=F32)


N_MIXER_IN = 9


def _mix_kernel(sinks_ref, x_ref, *refs, layer):
    mixer_in, (xg_ref, wq_ref, k_ref, v_ref, wo_ref, o_ref, *scratch) = (
        refs[:N_MIXER_IN], refs[N_MIXER_IN:])
    *mixer_scratch, att_ref = scratch
    x = _mixer_body(x_ref[...], pl.program_id(1), layer, sinks_ref, *mixer_in, *mixer_scratch)
    o_ref[...] = _xattn_body(x, xg_ref[layer:layer + 1, :], wq_ref, k_ref, v_ref, wo_ref, att_ref)


def _mix(x, cos_t, sin_t, batch, seq, layer, sinks, g, win, pw, ps, an, pn, wout, xg, wq, kv, wo):
    n, d = x.shape
    tm = ROW_TILE
    spt = seq // tm
    m = kv.shape[1]
    row_map = lambda b, j: (b * spt + j, 0)
    mixer_in = (cos_t, sin_t, g, win, pw, ps, an, pn, wout)
    assert len(mixer_in) == N_MIXER_IN
    return pl.pallas_call(
        functools.partial(_mix_kernel, layer=layer),
        grid=(batch, spt),
        in_specs=[pl.BlockSpec(memory_space=pltpu.SMEM),
                  pl.BlockSpec((tm, d), row_map),
                  pl.BlockSpec((tm, LANES), row_map),
                  pl.BlockSpec((tm, LANES), row_map),
                  _resident(g.shape), _resident(win.shape), _resident(pw.shape),
                  _resident(ps.shape), _resident(an.shape),
                  _resident(pn.shape), _resident(wout.shape),
                  _resident(xg.shape), _resident(wq.shape),
                  pl.BlockSpec((None, m, d), lambda b, j: (b, 0, 0)),
                  pl.BlockSpec((None, m, d), lambda b, j: (b, 0, 1)),
                  _resident(wo.shape)],
        out_specs=pl.BlockSpec((tm, d), row_map),
        out_shape=jax.ShapeDtypeStruct((n, d), F32),
        scratch_shapes=[pltpu.VMEM((N_KV_HEADS, BLOCK + tm, LANES), BF16),
                        pltpu.VMEM((N_KV_HEADS, BLOCK + tm, LANES), BF16),
                        pltpu.VMEM((POOL_TOP + tm, POOL_WIDTH), F32),
                        pltpu.VMEM((POOL_TOP + tm, POOL_WIDTH), F32),
                        pltpu.VMEM((POOL_TOP + tm, POOL_WIDTH - POOL_GROUP_DIM), F32),
                        pltpu.VMEM((POOL_TOP + tm, POOL_WIDTH - 2 * POOL_GROUP_DIM), F32),
                        pltpu.VMEM((tm, ATTN_WIDTH), F32),
                        pltpu.VMEM((tm, POOL_WIDTH), F32),
                        pltpu.VMEM((tm, d), BF16)],
        compiler_params=pltpu.CompilerParams(
            dimension_semantics=("arbitrary", "arbitrary"), vmem_limit_bytes=VMEM_LIMIT),
        name="mix",
    )(sinks, x, *mixer_in, xg, wq, kv, kv, wo)


def kernel(x, mem, positions, ffn1_norm, ffn1_w_gate, ffn1_w_up, ffn1_w_down, mix_norm, w_in,
           attn_sinks, pool_w, pool_scale, attn_out_norm, pool_out_norm, w_out, xattn_norm,
           mem_norm, xattn_wq, xattn_wkv, xattn_wo, ffn2_norm, ffn2_w_gate, ffn2_w_up,
           ffn2_w_down, final_norm):
    batch, seq, d = x.shape
    depth = w_in.shape[0]
    assert seq % ROW_TILE == 0 and ROW_TILE % BLOCK == 0 and d == D_MODEL
    n = batch * seq
    pool_w2 = pool_w.reshape(depth, POOL_WIDTH, POOL_GROUP_DIM)
    ffn1_w = (ffn1_w_gate, ffn1_w_up, ffn1_w_down)
    ffn2_w = (ffn2_w_gate, ffn2_w_up, ffn2_w_down)
    rest_w = (w_in, pool_w2, w_out, xattn_wq, xattn_wkv, xattn_wo) + ffn2_w

    xf = x.reshape(n, d)
    fg = final_norm.reshape(1, d)
    (cos_t, sin_t), w1 = _rope_tables(positions, casts=[(w, 0) for w in ffn1_w])
    for l in range(depth):
        xf, wl = _ffn(xf, ffn1_norm, l, *w1, fg, final=False, casts=[(w, l) for w in rest_w])
        win_b, pw_b, wout_b, wq_b, wkv_b, wo_b = wl[:6]
        kv = _mem_kv(mem, mem_norm, l, wkv_b)
        xf = _mix(xf, cos_t, sin_t, batch, seq, l, attn_sinks, mix_norm, win_b, pw_b,
                  pool_scale, attn_out_norm, pool_out_norm, wout_b, xattn_norm, wq_b, kv, wo_b)
        last = l == depth - 1
        xf, w1 = _ffn(xf, ffn2_norm, l, *wl[6:], fg, final=last,
                      casts=[] if last else [(w, l + 1) for w in ffn1_w])
    return xf.reshape(batch, seq, d)
```

```python
import functools

import jax
import jax.numpy as jnp
from jax import lax
from jax.experimental import pallas as pl
from jax.experimental.pallas import tpu as pltpu

F32 = jnp.float32
BF16 = jnp.bfloat16

D_MODEL = 1024
HEAD_DIM = 64
N_Q_HEADS = 8
N_KV_HEADS = 2
ATTN_WIDTH = N_Q_HEADS * HEAD_DIM
KV_WIDTH = N_KV_HEADS * HEAD_DIM
BLOCK = 128
POOL_WINDOWS = (2, 4, 8, 16)
POOL_GROUP_DIM = 128
POOL_WIDTH = 512
POOL_HALO = 16
POOL_TOP = 32
X_HEADS = 4
X_HEAD_DIM = 256
D_FF = 2816
FFN_RES = 0.5
EPS = 1e-6
ROPE_THETA = 10000.0
NEG = -1e30
LOG2_E = 1.4426950408889634
LANES = 128
BF16_SUBLANES = 16

ROW_TILE = 1024
MXU_TILE = 256
FF_CHUNK = 6 * MXU_TILE
FF_CHUNKS = tuple((lo, min(lo + FF_CHUNK, D_FF)) for lo in range(0, D_FF, FF_CHUNK))
VMEM_LIMIT = 56 * 1024 * 1024


def _rms(x, g):
    ms = jnp.mean(x * x, axis=-1, keepdims=True)
    return x * lax.rsqrt(ms + EPS) * g


def _resident(shape):
    nd = len(shape)
    return pl.BlockSpec(shape, lambda *_: (0,) * nd, pipeline_mode=pl.Buffered(1))


def _rope_table_kernel(pos_ref, freq_ref, *refs, n_casts):
    cast_in, (cos_ref, sin_ref, *cast_out) = refs[:n_casts], refs[n_casts:]
    nfreq = HEAD_DIM // 2
    q = pos_ref.shape[0] // (LANES // nfreq)
    group = lax.broadcasted_iota(jnp.int32, (q, LANES), 1) // nfreq
    freq = freq_ref[...]
    ang = pos_ref[0:q, :].astype(F32) * freq
    for k in range(1, LANES // nfreq):
        ang = jnp.where(group == k, pos_ref[k * q:(k + 1) * q, :].astype(F32) * freq, ang)
    even = group % 2 == 0
    for t, ref, rotate_half_sign in ((jnp.cos(ang), cos_ref, False), (jnp.sin(ang), sin_ref, True)):
        pairs = (jnp.where(even, t, pltpu.roll(t, nfreq, 1)),
                 jnp.where(even, pltpu.roll(t, LANES - nfreq, 1), t))
        swaps = tuple(pltpu.roll(b, 2 * nfreq, 1) for b in pairs)
        for k in range(LANES // nfreq):
            b, swapped = pairs[k % 2], swaps[k % 2]
            full = jnp.where(group < 2, b, swapped) if k < 2 else jnp.where(group < 2, swapped, b)
            if rotate_half_sign:
                full = jnp.where(even, -full, full)
            ref[k * q:(k + 1) * q, :] = full
    _cast_job(cast_in, cast_out)


def _rope_tables(positions, casts=()):
    n = positions.size
    inv_freq = ROPE_THETA ** (-jnp.arange(0, HEAD_DIM, 2, dtype=F32) / HEAD_DIM)
    freq = jnp.tile(inv_freq, LANES // (HEAD_DIM // 2)).reshape(1, LANES)
    tm = 2048
    c_in, c_out, c_shape, c_args = _cast_specs(casts, n // tm)
    res = pl.pallas_call(
        functools.partial(_rope_table_kernel, n_casts=len(casts)),
        grid=(n // tm,),
        in_specs=[pl.BlockSpec((tm, 1), lambda i: (i, 0)),
                  pl.BlockSpec((1, LANES), lambda i: (0, 0))] + c_in,
        out_specs=[pl.BlockSpec((tm, LANES), lambda i: (i, 0))] * 2 + c_out,
        out_shape=[jax.ShapeDtypeStruct((n, LANES), F32)] * 2 + c_shape,
        compiler_params=pltpu.CompilerParams(dimension_semantics=("arbitrary",)),
        name="rope_tables",
    )(positions.reshape(n, 1), freq, *c_args)
    return res[:2], res[2:]


def _cast_blocks(rows, nsteps):
    rb = -(-rows // nsteps)
    rb = -(-rb // BF16_SUBLANES) * BF16_SUBLANES
    while rows % rb:
        rb += BF16_SUBLANES
    return rb, rows // rb


def _cast_specs(casts, nsteps):
    in_specs, out_specs, out_shape, args = [], [], [], []
    for src, src_layer in casts:
        rows, cols = src.shape[1:]
        rb, nb = _cast_blocks(rows, nsteps)
        in_specs.append(pl.BlockSpec(
            (None, rb, cols), lambda i, sl=src_layer, nb=nb: (sl, jnp.minimum(i, nb - 1), 0)))
        out_specs.append(pl.BlockSpec((rb, cols), lambda i, nb=nb: (jnp.minimum(i, nb - 1), 0)))
        out_shape.append(jax.ShapeDtypeStruct((rows, cols), BF16))
        args.append(src)
    return in_specs, out_specs, out_shape, args


def _cast_job(cast_in, cast_out):
    for src, dst in zip(cast_in, cast_out):
        dst[...] = src[...].astype(BF16)


def _ffn_body(x, g, wg_ref, wu_ref, wd_ref):
    xn = _rms(x, g).astype(BF16)
    y = None
    for lo, hi in FF_CHUNKS:
        gate = jnp.dot(xn, wg_ref[:, lo:hi], preferred_element_type=F32)
        up = jnp.dot(xn, wu_ref[:, lo:hi], preferred_element_type=F32)
        h = (gate / (1.0 + jnp.exp(-gate)) * up).astype(BF16)
        part = jnp.dot(h, wd_ref[lo:hi, :], preferred_element_type=F32)
        y = part if y is None else y + part
    return x + FFN_RES * y


def _ffn_kernel(*refs, layer, final, n_casts):
    x_ref, g_ref, wg_ref, wu_ref, wd_ref, fg_ref = refs[:6]
    cast_in, (o_ref, *cast_out) = refs[6:6 + n_casts], refs[6 + n_casts:]
    out = _ffn_body(x_ref[...], g_ref[layer:layer + 1, :], wg_ref, wu_ref, wd_ref)
    if final:
        out = _rms(out, fg_ref[...])
    o_ref[...] = out
    _cast_job(cast_in, cast_out)


def _ffn(x, g, layer, wg, wu, wd, fg, final, casts=()):
    n, d = x.shape
    tm = ROW_TILE
    nsteps = n // tm
    c_in, c_out, c_shape, c_args = _cast_specs(casts, nsteps)
    in_specs = [pl.BlockSpec((tm, d), lambda i: (i, 0)),
                _resident(g.shape), _resident(wg.shape), _resident(wu.shape),
                _resident(wd.shape), _resident((1, d))] + c_in
    out_specs = [pl.BlockSpec((tm, d), lambda i: (i, 0))] + c_out
    out_shape = [jax.ShapeDtypeStruct((n, d), F32)] + c_shape
    args = [x, g, wg, wu, wd, fg] + c_args
    res = pl.pallas_call(
        functools.partial(_ffn_kernel, layer=layer, final=final, n_casts=len(casts)),
        grid=(nsteps,),
        in_specs=in_specs,
        out_specs=out_specs,
        out_shape=out_shape,
        compiler_params=pltpu.CompilerParams(
            dimension_semantics=("arbitrary",), vmem_limit_bytes=VMEM_LIMIT),
        name="ffn_final" if final else "ffn",
    )(*args)
    return res[0], res[1:]


def _rope(t, cos, sin_signed):
    lane = lax.broadcasted_iota(jnp.int32, t.shape, 1)
    first_half = (lane % HEAD_DIM) < (HEAD_DIM // 2)
    swapped = jnp.where(first_half,
                        pltpu.roll(t, LANES - HEAD_DIM // 2, 1),
                        pltpu.roll(t, HEAD_DIM // 2, 1))
    return t * cos + swapped * sin_signed


def _mixer_body(x_ref, j, layer, sinks_ref, cos_ref, sin_ref, g_ref, win_ref, pw_ref, ps_ref,
                an_ref, pn_ref, wout_ref, kd, vd, ub, l1, l2, l3, oa_ref, ob_ref):
    tm = x_ref.shape[0]
    nblk = tm // BLOCK
    group = N_Q_HEADS // N_KV_HEADS

    @pl.when(j == 0)
    def _():
        kd[:, 0:BLOCK, :] = jnp.zeros((N_KV_HEADS, BLOCK, LANES), BF16)
        vd[:, 0:BLOCK, :] = jnp.zeros((N_KV_HEADS, BLOCK, LANES), BF16)
        ub[0:POOL_TOP, :] = jnp.zeros((POOL_TOP, POOL_WIDTH), F32)

    x = x_ref[...]
    lrow = slice(layer, layer + 1)
    h = _rms(x, g_ref[lrow, :]).astype(BF16)
    proj = jnp.dot(h, win_ref[...], preferred_element_type=F32)
    cos = cos_ref[...]
    sin = sin_ref[...]

    low_t = lax.broadcasted_iota(jnp.int32, (tm, LANES), 1) < HEAD_DIM
    k = _rope(proj[:, ATTN_WIDTH:ATTN_WIDTH + KV_WIDTH], cos, sin)
    v = proj[:, ATTN_WIDTH + KV_WIDTH:ATTN_WIDTH + 2 * KV_WIDTH]
    kr = pltpu.roll(k, HEAD_DIM, 1)
    vr = pltpu.roll(v, HEAD_DIM, 1)
    kd[0, BLOCK:BLOCK + tm, :] = jnp.where(low_t, k, kr).astype(BF16)
    kd[1, BLOCK:BLOCK + tm, :] = jnp.where(low_t, kr, k).astype(BF16)
    vd[0, BLOCK:BLOCK + tm, :] = jnp.where(low_t, v, vr).astype(BF16)
    vd[1, BLOCK:BLOCK + tm, :] = jnp.where(low_t, vr, v).astype(BF16)
    ub[POOL_TOP:POOL_TOP + tm, :] = proj[:, ATTN_WIDTH + 2 * KV_WIDTH:]

    low = lax.broadcasted_iota(jnp.int32, (BLOCK, LANES), 1) < HEAD_DIM
    tri_t = (lax.broadcasted_iota(jnp.int32, (BLOCK, BLOCK), 0)
             <= lax.broadcasted_iota(jnp.int32, (BLOCK, BLOCK), 1))
    tri_tb = tri_t.astype(BF16)
    first = j == 0
    scale = HEAD_DIM ** -0.5 * LOG2_E
    zero_bf = jnp.zeros((BLOCK, LANES), BF16)

    q_pairs = [(_rope(proj[:, p * LANES:(p + 1) * LANES], cos, sin) * scale).astype(BF16)
               for p in range(ATTN_WIDTH // LANES)]
    def scores(kvh, b):
        rows = slice(b * BLOCK, (b + 1) * BLOCK)
        pieces = []
        for idx in range(group):
            qb = q_pairs[(kvh * group + idx) // 2][rows, :]
            pieces.append(jnp.where(low if idx % 2 == 0 else ~low, qb, zero_bf))
        q4 = jnp.concatenate(pieces, axis=0)
        return lax.dot_general(kd[kvh, b * BLOCK:(b + 2) * BLOCK, :], q4, (((1,), (1,)), ((), ())),
                               preferred_element_type=F32)

    def attend(kvh, b, s4):
        rows = slice(b * BLOCK, (b + 1) * BLOCK)
        probs = []
        for idx in range(group):
            hcols = slice(idx * BLOCK, (idx + 1) * BLOCK)
            s_prev = s4[0:BLOCK, hcols]
            if b == 0:
                s_prev = jnp.where(first, NEG, s_prev)
            s = jnp.where(tri_t, s4[BLOCK:2 * BLOCK, hcols], s_prev)
            sink = sinks_ref[layer, kvh * group + idx] * LOG2_E
            m = jnp.maximum(jnp.max(s, axis=0, keepdims=True), sink)
            ex = jnp.exp2(s - m)
            den = jnp.sum(ex, axis=0, keepdims=True) + jnp.exp2(sink - m)
            pr = (ex * (1.0 / den)).astype(BF16)
            p_cur = pr * tri_tb
            probs.append(jnp.concatenate([pr - p_cur, p_cur], axis=0))
        p4 = jnp.concatenate(probs, axis=1)
        o4 = lax.dot_general(p4, vd[kvh, b * BLOCK:(b + 2) * BLOCK, :], (((0,), (0,)), ((), ())),
                             preferred_element_type=F32)
        for pp in range(group // 2):
            pcol = (kvh * (group // 2) + pp) * LANES
            oa_ref[rows, pcol:pcol + LANES] = jnp.where(
                low, o4[(2 * pp) * BLOCK:(2 * pp + 1) * BLOCK, :],
                o4[(2 * pp + 1) * BLOCK:(2 * pp + 2) * BLOCK, :])

    units = [(kvh, b) for kvh in range(N_KV_HEADS) for b in range(nblk)]
    s_next = scores(*units[0])
    for u, unit in enumerate(units):
        s_cur = s_next
        if u + 1 < len(units):
            s_next = scores(*units[u + 1])
        attend(*unit, s_cur)

    n1 = tm + POOL_TOP - 8
    l1[8:8 + n1, :] = ub[8:8 + n1, :] + ub[7:7 + n1, :]
    n2 = tm + POOL_TOP - 16
    c2 = POOL_GROUP_DIM
    l2[16:16 + n2, :] = l1[16:16 + n2, c2:] + l1[14:14 + n2, c2:]
    n3 = tm + POOL_TOP - 24
    l3[24:24 + n3, :] = l2[24:24 + n3, c2:] + l2[20:20 + n3, c2:]
    top = POOL_TOP
    sums = [l1[top:top + tm, 0:c2], l2[top:top + tm, 0:c2], l3[top:top + tm, 0:c2],
            l3[top:top + tm, c2:] + l3[top - 8:top - 8 + tm, c2:]]
    row = lax.broadcasted_iota(jnp.int32, (POOL_HALO, POOL_GROUP_DIM), 0) + j * tm
    pooled = []
    for gi, w in enumerate(POOL_WINDOWS):
        cols = slice(gi * POOL_GROUP_DIM, (gi + 1) * POOL_GROUP_DIM)
        cnt = jnp.minimum(row + 1, w).astype(F32)
        mean = jnp.concatenate([sums[gi][0:POOL_HALO] / cnt, sums[gi][POOL_HALO:] * (1.0 / w)],
                               axis=0)
        pooled.append((mean - ub[top:top + tm, cols]).astype(BF16))
    zero_w = jnp.zeros((POOL_GROUP_DIM, POOL_GROUP_DIM), BF16)
    for gp in range(0, len(POOL_WINDOWS), 2):
        ca = slice(gp * POOL_GROUP_DIM, (gp + 1) * POOL_GROUP_DIM)
        cb = slice((gp + 1) * POOL_GROUP_DIM, (gp + 2) * POOL_GROUP_DIM)
        w_pair = jnp.concatenate(
            [jnp.concatenate([pw_ref[ca, :], zero_w], axis=1),
             jnp.concatenate([zero_w, pw_ref[cb, :]], axis=1)], axis=0)
        mixed = jnp.dot(jnp.concatenate(pooled[gp:gp + 2], axis=1), w_pair,
                        preferred_element_type=F32)
        cab = slice(gp * POOL_GROUP_DIM, (gp + 2) * POOL_GROUP_DIM)
        ob_ref[:, cab] = mixed * ps_ref[lrow, cab]

    kd[:, 0:BLOCK, :] = kd[:, tm:tm + BLOCK, :]
    vd[:, 0:BLOCK, :] = vd[:, tm:tm + BLOCK, :]
    ub[POOL_TOP - POOL_HALO:POOL_TOP, :] = ub[tm + POOL_TOP - POOL_HALO:tm + POOL_TOP, :]

    na = _rms(oa_ref[...], an_ref[lrow, :]).astype(BF16)
    nb = _rms(ob_ref[...], pn_ref[lrow, :]).astype(BF16)
    y = jnp.dot(na, wout_ref[0:ATTN_WIDTH, :], preferred_element_type=F32)
    y = y + jnp.dot(nb, wout_ref[ATTN_WIDTH:, :], preferred_element_type=F32)
    return x + y


def _mem_kv_kernel(mem_ref, g_ref, wkv_ref, kv_ref, *, layer):
    mn = _rms(mem_ref[...], g_ref[layer:layer + 1, :]).astype(BF16)
    kv_ref[...] = jnp.dot(mn, wkv_ref[...], preferred_element_type=F32).astype(BF16)


def _mem_kv(mem, g, layer, wkv):
    b, m, d = mem.shape
    return pl.pallas_call(
        functools.partial(_mem_kv_kernel, layer=layer),
        grid=(b,),
        in_specs=[pl.BlockSpec((None, m, d), lambda i: (i, 0, 0)),
                  _resident(g.shape), _resident(wkv.shape)],
        out_specs=pl.BlockSpec((None, m, 2 * d), lambda i: (i, 0, 0)),
        out_shape=jax.ShapeDtypeStruct((b, m, 2 * d), BF16),
        compiler_params=pltpu.CompilerParams(
            dimension_semantics=("arbitrary",), vmem_limit_bytes=VMEM_LIMIT),
        name="mem_kv",
    )(mem, g, wkv)


def _xattn_body(x, g, wq_ref, k_ref, v_ref, wo_ref, att_ref):
    h = _rms(x, g).astype(BF16)
    q = jnp.dot(h, wq_ref[...], preferred_element_type=F32) * (X_HEAD_DIM ** -0.5 * LOG2_E)
    q = q.astype(BF16)

    def head_cols(hd):
        return slice(hd * X_HEAD_DIM, (hd + 1) * X_HEAD_DIM)

    def scores(hd):
        return lax.dot_general(k_ref[:, head_cols(hd)], q[:, head_cols(hd)],
                               (((1,), (1,)), ((), ())), preferred_element_type=F32)

    s_next = scores(0)
    for hd in range(X_HEADS):
        s = s_next
        if hd + 1 < X_HEADS:
            s_next = scores(hd + 1)
        m = jnp.max(s, axis=0, keepdims=True)
        ex = jnp.exp2(s - m)
        pr = (ex * (1.0 / jnp.sum(ex, axis=0, keepdims=True))).astype(BF16)
        att_ref[:, head_cols(hd)] = lax.dot_general(
            pr, v_ref[:, head_cols(hd)], (((0,), (0,)), ((), ())),
            preferred_element_type=F32).astype(BF16)
    return x + jnp.dot(att_ref[...], wo_ref[...], preferred_element_type=F32)


N_MIXER_IN = 9


def _mix_kernel(sinks_ref, x_ref, *refs, layer):
    mixer_in, (xg_ref, wq_ref, k_ref, v_ref, wo_ref, o_ref, *scratch) = (
        refs[:N_MIXER_IN], refs[N_MIXER_IN:])
    *mixer_scratch, att_ref = scratch
    x = _mixer_body(x_ref, pl.program_id(1), layer, sinks_ref, *mixer_in, *mixer_scratch)
    o_ref[...] = _xattn_body(x, xg_ref[layer:layer + 1, :], wq_ref, k_ref, v_ref, wo_ref, att_ref)


def _mix(x, cos_t, sin_t, batch, seq, layer, sinks, g, win, pw, ps, an, pn, wout, xg, wq, kv, wo):
    n, d = x.shape
    tm = ROW_TILE
    spt = seq // tm
    m = kv.shape[1]
    row_map = lambda b, j: (b * spt + j, 0)
    mixer_in = (cos_t, sin_t, g, win, pw, ps, an, pn, wout)
    assert len(mixer_in) == N_MIXER_IN
    return pl.pallas_call(
        functools.partial(_mix_kernel, layer=layer),
        grid=(batch, spt),
        in_specs=[pl.BlockSpec(memory_space=pltpu.SMEM),
                  pl.BlockSpec((tm, d), row_map),
                  pl.BlockSpec((tm, LANES), row_map),
                  pl.BlockSpec((tm, LANES), row_map),
                  _resident(g.shape), _resident(win.shape), _resident(pw.shape),
                  _resident(ps.shape), _resident(an.shape),
                  _resident(pn.shape), _resident(wout.shape),
                  _resident(xg.shape), _resident(wq.shape),
                  pl.BlockSpec((None, m, d), lambda b, j: (b, 0, 0)),
                  pl.BlockSpec((None, m, d), lambda b, j: (b, 0, 1)),
                  _resident(wo.shape)],
        out_specs=pl.BlockSpec((tm, d), row_map),
        out_shape=jax.ShapeDtypeStruct((n, d), F32),
        scratch_shapes=[pltpu.VMEM((N_KV_HEADS, BLOCK + tm, LANES), BF16),
                        pltpu.VMEM((N_KV_HEADS, BLOCK + tm, LANES), BF16),
                        pltpu.VMEM((POOL_TOP + tm, POOL_WIDTH), F32),
                        pltpu.VMEM((POOL_TOP + tm, POOL_WIDTH), F32),
                        pltpu.VMEM((POOL_TOP + tm, POOL_WIDTH - POOL_GROUP_DIM), F32),
                        pltpu.VMEM((POOL_TOP + tm, POOL_WIDTH - 2 * POOL_GROUP_DIM), F32),
                        pltpu.VMEM((tm, ATTN_WIDTH), F32),
                        pltpu.VMEM((tm, POOL_WIDTH), F32),
                        pltpu.VMEM((tm, d), BF16)],
        compiler_params=pltpu.CompilerParams(
            dimension_semantics=("arbitrary", "arbitrary"), vmem_limit_bytes=VMEM_LIMIT),
        name="mix",
    )(sinks, x, *mixer_in, xg, wq, kv, kv, wo)


def kernel(x, mem, positions, ffn1_norm, ffn1_w_gate, ffn1_w_up, ffn1_w_down, mix_norm, w_in,
           attn_sinks, pool_w, pool_scale, attn_out_norm, pool_out_norm, w_out, xattn_norm,
           mem_norm, xattn_wq, xattn_wkv, xattn_wo, ffn2_norm, ffn2_w_gate, ffn2_w_up,
           ffn2_w_down, final_norm):
    batch, seq, d = x.shape
    depth = w_in.shape[0]
    assert seq % ROW_TILE == 0 and ROW_TILE % BLOCK == 0 and d == D_MODEL
    n = batch * seq
    pool_w2 = pool_w.reshape(depth, POOL_WIDTH, POOL_GROUP_DIM)
    ffn1_w = (ffn1_w_gate, ffn1_w_up, ffn1_w_down)
    ffn2_w = (ffn2_w_gate, ffn2_w_up, ffn2_w_down)
    rest_w = (w_in, pool_w2, w_out, xattn_wq, xattn_wkv, xattn_wo) + ffn2_w

    xf = x.reshape(n, d)
    fg = final_norm.reshape(1, d)
    (cos_t, sin_t), w1 = _rope_tables(positions, casts=[(w, 0) for w in ffn1_w])
    for l in range(depth):
        xf, wl = _ffn(xf, ffn1_norm, l, *w1, fg, final=False, casts=[(w, l) for w in rest_w])
        win_b, pw_b, wout_b, wq_b, wkv_b, wo_b = wl[:6]
        kv = _mem_kv(mem, mem_norm, l, wkv_b)
        xf = _mix(xf, cos_t, sin_t, batch, seq, l, attn_sinks, mix_norm, win_b, pw_b,
                  pool_scale, attn_out_norm, pool_out_norm, wout_b, xattn_norm, wq_b, kv, wo_b)
        last = l == depth - 1
        xf, w1 = _ffn(xf, ffn2_norm, l, *wl[6:], fg, final=last,
                      casts=[] if last else [(w, l + 1) for w in ffn1_w])
    return xf.reshape(batch, seq, d)
```

```python
import functools

import jax
import jax.numpy as jnp
from jax import lax
from jax.experimental import pallas as pl
from jax.experimental.pallas import tpu as pltpu

F32 = jnp.float32
BF16 = jnp.bfloat16

D_MODEL = 1024
HEAD_DIM = 64
N_Q_HEADS = 8
N_KV_HEADS = 2
ATTN_WIDTH = N_Q_HEADS * HEAD_DIM
KV_WIDTH = N_KV_HEADS * HEAD_DIM
BLOCK = 128
POOL_WINDOWS = (2, 4, 8, 16)
POOL_GROUP_DIM = 128
POOL_WIDTH = 512
POOL_HALO = 16
POOL_TOP = 32
X_HEADS = 4
X_HEAD_DIM = 256
D_FF = 2816
FFN_RES = 0.5
EPS = 1e-6
ROPE_THETA = 10000.0
NEG = -1e30
LOG2_E = 1.4426950408889634
LANES = 128
BF16_SUBLANES = 16

ROW_TILE = 1024
MXU_TILE = 256
FF_CHUNK = 6 * MXU_TILE
FF_CHUNKS = tuple((lo, min(lo + FF_CHUNK, D_FF)) for lo in range(0, D_FF, FF_CHUNK))
VMEM_LIMIT = 56 * 1024 * 1024


def _rms(x, g):
    ms = jnp.mean(x * x, axis=-1, keepdims=True)
    return x * lax.rsqrt(ms + EPS) * g


def _resident(shape):
    nd = len(shape)
    return pl.BlockSpec(shape, lambda *_: (0,) * nd, pipeline_mode=pl.Buffered(1))


def _rope_table_kernel(pos_ref, freq_ref, *refs, n_casts):
    cast_in, (cos_ref, sin_ref, *cast_out) = refs[:n_casts], refs[n_casts:]
    nfreq = HEAD_DIM // 2
    ngroups = LANES // nfreq
    pos = pos_ref[...].astype(F32)
    q = pos.shape[0] * LANES // ngroups
    pos_t = jnp.concatenate([pos, jnp.zeros((LANES - pos.shape[0], LANES), F32)], axis=0).T
    freq = freq_ref[...]
    group_s = lax.broadcasted_iota(jnp.int32, (LANES, LANES), 1) // nfreq
    slabs = []
    for a in range(q // LANES):
        ang_a = pos_t[:, a:a + 1] * freq
        for k in range(1, ngroups):
            c = k * (q // LANES) + a
            ang_a = jnp.where(group_s == k, pos_t[:, c:c + 1] * freq, ang_a)
        slabs.append(ang_a)
    ang = jnp.concatenate(slabs, axis=0)
    group = lax.broadcasted_iota(jnp.int32, (q, LANES), 1) // nfreq
    even = group % 2 == 0
    for t, ref, rotate_half_sign in ((jnp.cos(ang), cos_ref, False), (jnp.sin(ang), sin_ref, True)):
        pairs = (jnp.where(even, t, pltpu.roll(t, nfreq, 1)),
                 jnp.where(even, pltpu.roll(t, LANES - nfreq, 1), t))
        swaps = tuple(pltpu.roll(b, 2 * nfreq, 1) for b in pairs)
        for k in range(LANES // nfreq):
            b, swapped = pairs[k % 2], swaps[k % 2]
            full = jnp.where(group < 2, b, swapped) if k < 2 else jnp.where(group < 2, swapped, b)
            if rotate_half_sign:
                full = jnp.where(even, -full, full)
            ref[k * q:(k + 1) * q, :] = full
    _cast_job(cast_in, cast_out)


def _rope_tables(positions, casts=()):
    n = positions.size
    inv_freq = ROPE_THETA ** (-jnp.arange(0, HEAD_DIM, 2, dtype=F32) / HEAD_DIM)
    freq = jnp.tile(inv_freq, LANES // (HEAD_DIM // 2)).reshape(1, LANES)
    tm = 2048
    c_in, c_out, c_shape, c_args = _cast_specs(casts, n // tm)
    res = pl.pallas_call(
        functools.partial(_rope_table_kernel, n_casts=len(casts)),
        grid=(n // tm,),
        in_specs=[pl.BlockSpec((tm // LANES, LANES), lambda i: (i, 0)),
                  pl.BlockSpec((1, LANES), lambda i: (0, 0))] + c_in,
        out_specs=[pl.BlockSpec((tm, LANES), lambda i: (i, 0))] * 2 + c_out,
        out_shape=[jax.ShapeDtypeStruct((n, LANES), F32)] * 2 + c_shape,
        compiler_params=pltpu.CompilerParams(dimension_semantics=("arbitrary",)),
        name="rope_tables",
    )(positions.reshape(n // LANES, LANES), freq, *c_args)
    return res[:2], res[2:]


def _cast_blocks(rows, nsteps):
    rb = -(-rows // nsteps)
    rb = -(-rb // BF16_SUBLANES) * BF16_SUBLANES
    while rows % rb:
        rb += BF16_SUBLANES
    return rb, rows // rb


def _cast_specs(casts, nsteps):
    in_specs, out_specs, out_shape, args = [], [], [], []
    for src, src_layer in casts:
        rows, cols = src.shape[1:]
        rb, nb = _cast_blocks(rows, nsteps)
        in_specs.append(pl.BlockSpec(
            (None, rb, cols), lambda i, sl=src_layer, nb=nb: (sl, jnp.minimum(i, nb - 1), 0)))
        out_specs.append(pl.BlockSpec((rb, cols), lambda i, nb=nb: (jnp.minimum(i, nb - 1), 0)))
        out_shape.append(jax.ShapeDtypeStruct((rows, cols), BF16))
        args.append(src)
    return in_specs, out_specs, out_shape, args


def _cast_job(cast_in, cast_out):
    for src, dst in zip(cast_in, cast_out):
        dst[...] = src[...].astype(BF16)


def _ffn_body(x, g, wg_ref, wu_ref, wd_ref):
    xn = _rms(x, g).astype(BF16)
    y = None
    for lo, hi in FF_CHUNKS:
        gate = jnp.dot(xn, wg_ref[:, lo:hi], preferred_element_type=F32)
        up = jnp.dot(xn, wu_ref[:, lo:hi], preferred_element_type=F32)
        h = (gate / (1.0 + jnp.exp(-gate)) * up).astype(BF16)
        part = jnp.dot(h, wd_ref[lo:hi, :], preferred_element_type=F32)
        y = part if y is None else y + part
    return x + FFN_RES * y


def _ffn_kernel(*refs, layer, final, n_casts):
    x_ref, g_ref, wg_ref, wu_ref, wd_ref, fg_ref = refs[:6]
    cast_in, (o_ref, *cast_out) = refs[6:6 + n_casts], refs[6 + n_casts:]
    out = _ffn_body(x_ref[...], g_ref[layer:layer + 1, :], wg_ref, wu_ref, wd_ref)
    if final:
        out = _rms(out, fg_ref[...])
    o_ref[...] = out
    _cast_job(cast_in, cast_out)


def _ffn(x, g, layer, wg, wu, wd, fg, final, casts=()):
    n, d = x.shape
    tm = ROW_TILE
    nsteps = n // tm
    c_in, c_out, c_shape, c_args = _cast_specs(casts, nsteps)
    in_specs = [pl.BlockSpec((tm, d), lambda i: (i, 0)),
                _resident(g.shape), _resident(wg.shape), _resident(wu.shape),
                _resident(wd.shape), _resident((1, d))] + c_in
    out_specs = [pl.BlockSpec((tm, d), lambda i: (i, 0))] + c_out
    out_shape = [jax.ShapeDtypeStruct((n, d), F32)] + c_shape
    args = [x, g, wg, wu, wd, fg] + c_args
    res = pl.pallas_call(
        functools.partial(_ffn_kernel, layer=layer, final=final, n_casts=len(casts)),
        grid=(nsteps,),
        in_specs=in_specs,
        out_specs=out_specs,
        out_shape=out_shape,
        compiler_params=pltpu.CompilerParams(
            dimension_semantics=("arbitrary",), vmem_limit_bytes=VMEM_LIMIT),
        name="ffn_final" if final else "ffn",
    )(*args)
    return res[0], res[1:]


def _rope(t, cos, sin_signed):
    lane = lax.broadcasted_iota(jnp.int32, t.shape, 1)
    first_half = (lane % HEAD_DIM) < (HEAD_DIM // 2)
    swapped = jnp.where(first_half,
                        pltpu.roll(t, LANES - HEAD_DIM // 2, 1),
                        pltpu.roll(t, HEAD_DIM // 2, 1))
    return t * cos + swapped * sin_signed


def _mixer_body(x_ref, j, layer, sinks_ref, cos_ref, sin_ref, g_ref, win_ref, pw_ref, ps_ref,
                an_ref, pn_ref, wout_ref, kd, vd, ub, l1, l2, l3, oa_ref, ob_ref):
    tm = x_ref.shape[0]
    nblk = tm // BLOCK
    group = N_Q_HEADS // N_KV_HEADS

    @pl.when(j == 0)
    def _():
        kd[:, 0:BLOCK, :] = jnp.zeros((N_KV_HEADS, BLOCK, LANES), BF16)
        vd[:, 0:BLOCK, :] = jnp.zeros((N_KV_HEADS, BLOCK, LANES), BF16)
        ub[0:POOL_TOP, :] = jnp.zeros((POOL_TOP, POOL_WIDTH), F32)

    x = x_ref[...]
    lrow = slice(layer, layer + 1)
    h = _rms(x, g_ref[lrow, :]).astype(BF16)
    proj = jnp.dot(h, win_ref[...], preferred_element_type=F32)
    cos = cos_ref[...]
    sin = sin_ref[...]

    low_t = lax.broadcasted_iota(jnp.int32, (tm, LANES), 1) < HEAD_DIM
    k = _rope(proj[:, ATTN_WIDTH:ATTN_WIDTH + KV_WIDTH], cos, sin)
    v = proj[:, ATTN_WIDTH + KV_WIDTH:ATTN_WIDTH + 2 * KV_WIDTH]
    kr = pltpu.roll(k, HEAD_DIM, 1)
    vr = pltpu.roll(v, HEAD_DIM, 1)
    kd[0, BLOCK:BLOCK + tm, :] = jnp.where(low_t, k, kr).astype(BF16)
    kd[1, BLOCK:BLOCK + tm, :] = jnp.where(low_t, kr, k).astype(BF16)
    vd[0, BLOCK:BLOCK + tm, :] = jnp.where(low_t, v, vr).astype(BF16)
    vd[1, BLOCK:BLOCK + tm, :] = jnp.where(low_t, vr, v).astype(BF16)
    ub[POOL_TOP:POOL_TOP + tm, :] = proj[:, ATTN_WIDTH + 2 * KV_WIDTH:]

    low = lax.broadcasted_iota(jnp.int32, (BLOCK, LANES), 1) < HEAD_DIM
    tri_t = (lax.broadcasted_iota(jnp.int32, (BLOCK, BLOCK), 0)
             <= lax.broadcasted_iota(jnp.int32, (BLOCK, BLOCK), 1))
    tri_tb = tri_t.astype(BF16)
    first = j == 0
    scale = HEAD_DIM ** -0.5 * LOG2_E
    zero_bf = jnp.zeros((BLOCK, LANES), BF16)

    q_pairs = [(_rope(proj[:, p * LANES:(p + 1) * LANES], cos, sin) * scale).astype(BF16)
               for p in range(ATTN_WIDTH // LANES)]
    def scores(kvh, b):
        rows = slice(b * BLOCK, (b + 1) * BLOCK)
        pieces = []
        for idx in range(group):
            qb = q_pairs[(kvh * group + idx) // 2][rows, :]
            pieces.append(jnp.where(low if idx % 2 == 0 else ~low, qb, zero_bf))
        q4 = jnp.concatenate(pieces, axis=0)
        return lax.dot_general(kd[kvh, b * BLOCK:(b + 2) * BLOCK, :], q4, (((1,), (1,)), ((), ())),
                               preferred_element_type=F32)

    def attend(kvh, b, s4):
        rows = slice(b * BLOCK, (b + 1) * BLOCK)
        probs = []
        for idx in range(group):
            hcols = slice(idx * BLOCK, (idx + 1) * BLOCK)
            s_prev = s4[0:BLOCK, hcols]
            if b == 0:
                s_prev = jnp.where(first, NEG, s_prev)
            s = jnp.where(tri_t, s4[BLOCK:2 * BLOCK, hcols], s_prev)
            sink = sinks_ref[layer, kvh * group + idx] * LOG2_E
            m = jnp.maximum(jnp.max(s, axis=0, keepdims=True), sink)
            ex = jnp.exp2(s - m)
            den = jnp.sum(ex, axis=0, keepdims=True) + jnp.exp2(sink - m)
            pr = (ex * (1.0 / den)).astype(BF16)
            p_cur = pr * tri_tb
            probs.append(jnp.concatenate([pr - p_cur, p_cur], axis=0))
        p4 = jnp.concatenate(probs, axis=1)
        o4 = lax.dot_general(p4, vd[kvh, b * BLOCK:(b + 2) * BLOCK, :], (((0,), (0,)), ((), ())),
                             preferred_element_type=F32)
        for pp in range(group // 2):
            pcol = (kvh * (group // 2) + pp) * LANES
            oa_ref[rows, pcol:pcol + LANES] = jnp.where(
                low, o4[(2 * pp) * BLOCK:(2 * pp + 1) * BLOCK, :],
                o4[(2 * pp + 1) * BLOCK:(2 * pp + 2) * BLOCK, :])

    units = [(kvh, b) for kvh in range(N_KV_HEADS) for b in range(nblk)]
    s_next = scores(*units[0])
    for u, unit in enumerate(units):
        s_cur = s_next
        if u + 1 < len(units):
            s_next = scores(*units[u + 1])
        attend(*unit, s_cur)

    n1 = tm + POOL_TOP - 8
    l1[8:8 + n1, :] = ub[8:8 + n1, :] + ub[7:7 + n1, :]
    n2 = tm + POOL_TOP - 16
    c2 = POOL_GROUP_DIM
    l2[16:16 + n2, :] = l1[16:16 + n2, c2:] + l1[14:14 + n2, c2:]
    n3 = tm + POOL_TOP - 24
    l3[24:24 + n3, :] = l2[24:24 + n3, c2:] + l2[20:20 + n3, c2:]
    top = POOL_TOP
    sums = [l1[top:top + tm, 0:c2], l2[top:top + tm, 0:c2], l3[top:top + tm, 0:c2],
            l3[top:top + tm, c2:] + l3[top - 8:top - 8 + tm, c2:]]
    row = lax.broadcasted_iota(jnp.int32, (POOL_HALO, POOL_GROUP_DIM), 0) + j * tm
    pooled = []
    for gi, w in enumerate(POOL_WINDOWS):
        cols = slice(gi * POOL_GROUP_DIM, (gi + 1) * POOL_GROUP_DIM)
        cnt = jnp.minimum(row + 1, w).astype(F32)
        mean = jnp.concatenate([sums[gi][0:POOL_HALO] / cnt, sums[gi][POOL_HALO:] * (1.0 / w)],
                               axis=0)
        pooled.append((mean - ub[top:top + tm, cols]).astype(BF16))
    zero_w = jnp.zeros((POOL_GROUP_DIM, POOL_GROUP_DIM), BF16)
    for gp in range(0, len(POOL_WINDOWS), 2):
        ca = slice(gp * POOL_GROUP_DIM, (gp + 1) * POOL_GROUP_DIM)
        cb = slice((gp + 1) * POOL_GROUP_DIM, (gp + 2) * POOL_GROUP_DIM)
        w_pair = jnp.concatenate(
            [jnp.concatenate([pw_ref[ca, :], zero_w], axis=1),
             jnp.concatenate([zero_w, pw_ref[cb, :]], axis=1)], axis=0)
        mixed = jnp.dot(jnp.concatenate(pooled[gp:gp + 2], axis=1), w_pair,
                        preferred_element_type=F32)
        cab = slice(gp * POOL_GROUP_DIM, (gp + 2) * POOL_GROUP_DIM)
        ob_ref[:, cab] = mixed * ps_ref[lrow, cab]

    kd[:, 0:BLOCK, :] = kd[:, tm:tm + BLOCK, :]
    vd[:, 0:BLOCK, :] = vd[:, tm:tm + BLOCK, :]
    ub[POOL_TOP - POOL_HALO:POOL_TOP, :] = ub[tm + POOL_TOP - POOL_HALO:tm + POOL_TOP, :]

    na = _rms(oa_ref[...], an_ref[lrow, :]).astype(BF16)
    nb = _rms(ob_ref[...], pn_ref[lrow, :]).astype(BF16)
    y = jnp.dot(na, wout_ref[0:ATTN_WIDTH, :], preferred_element_type=F32)
    y = y + jnp.dot(nb, wout_ref[ATTN_WIDTH:, :], preferred_element_type=F32)
    return x + y


def _mem_kv_kernel(mem_ref, g_ref, wkv_ref, kv_ref, *, layer):
    mn = _rms(mem_ref[...], g_ref[layer:layer + 1, :]).astype(BF16)
    kv_ref[...] = jnp.dot(mn, wkv_ref[...], preferred_element_type=F32).astype(BF16)


def _mem_kv(mem, g, layer, wkv):
    b, m, d = mem.shape
    return pl.pallas_call(
        functools.partial(_mem_kv_kernel, layer=layer),
        grid=(b,),
        in_specs=[pl.BlockSpec((None, m, d), lambda i: (i, 0, 0)),
                  _resident(g.shape), _resident(wkv.shape)],
        out_specs=pl.BlockSpec((None, m, 2 * d), lambda i: (i, 0, 0)),
        out_shape=jax.ShapeDtypeStruct((b, m, 2 * d), BF16),
        compiler_params=pltpu.CompilerParams(
            dimension_semantics=("arbitrary",), vmem_limit_bytes=VMEM_LIMIT),
        name="mem_kv",
    )(mem, g, wkv)


def _xattn_body(x, g, wq_ref, k_ref, v_ref, wo_ref, att_ref):
    h = _rms(x, g).astype(BF16)
    q = jnp.dot(h, wq_ref[...], preferred_element_type=F32) * (X_HEAD_DIM ** -0.5 * LOG2_E)
    q = q.astype(BF16)

    def head_cols(hd):
        return slice(hd * X_HEAD_DIM, (hd + 1) * X_HEAD_DIM)

    def scores(hd):
        return lax.dot_general(k_ref[:, head_cols(hd)], q[:, head_cols(hd)],
                               (((1,), (1,)), ((), ())), preferred_element_type=F32)

    s_next = scores(0)
    for hd in range(X_HEADS):
        s = s_next
        if hd + 1 < X_HEADS:
            s_next = scores(hd + 1)
        m = jnp.max(s, axis=0, keepdims=True)
        ex = jnp.exp2(s - m)
        pr = (ex * (1.0 / jnp.sum(ex, axis=0, keepdims=True))).astype(BF16)
        att_ref[:, head_cols(hd)] = lax.dot_general(
            pr, v_ref[:, head_cols(hd)], (((0,), (0,)), ((), ())),
            preferred_element_type=F32).astype(BF16)
    return x + jnp.dot(att_ref[...], wo_ref[...], preferred_element_type=F32)


N_MIXER_IN = 9


def _mix_kernel(sinks_ref, x_ref, *refs, layer):
    mixer_in, (xg_ref, wq_ref, k_ref, v_ref, wo_ref, o_ref, *scratch) = (
        refs[:N_MIXER_IN], refs[N_MIXER_IN:])
    *mixer_scratch, att_ref = scratch
    x = _mixer_body(x_ref, pl.program_id(1), layer, sinks_ref, *mixer_in, *mixer_scratch)
    o_ref[...] = _xattn_body(x, xg_ref[layer:layer + 1, :], wq_ref, k_ref, v_ref, wo_ref, att_ref)


def _mix(x, cos_t, sin_t, batch, seq, layer, sinks, g, win, pw, ps, an, pn, wout, xg, wq, kv, wo):
    n, d = x.shape
    tm = ROW_TILE
    spt = seq // tm
    m = kv.shape[1]
    row_map = lambda b, j: (b * spt + j, 0)
    mixer_in = (cos_t, sin_t, g, win, pw, ps, an, pn, wout)
    assert len(mixer_in) == N_MIXER_IN
    return pl.pallas_call(
        functools.partial(_mix_kernel, layer=layer),
        grid=(batch, spt),
        in_specs=[pl.BlockSpec(memory_space=pltpu.SMEM),
                  pl.BlockSpec((tm, d), row_map),
                  pl.BlockSpec((tm, LANES), row_map),
                  pl.BlockSpec((tm, LANES), row_map),
                  _resident(g.shape), _resident(win.shape), _resident(pw.shape),
                  _resident(ps.shape), _resident(an.shape),
                  _resident(pn.shape), _resident(wout.shape),
                  _resident(xg.shape), _resident(wq.shape),
                  pl.BlockSpec((None, m, d), lambda b, j: (b, 0, 0)),
                  pl.BlockSpec((None, m, d), lambda b, j: (b, 0, 1)),
                  _resident(wo.shape)],
        out_specs=pl.BlockSpec((tm, d), row_map),
        out_shape=jax.ShapeDtypeStruct((n, d), F32),
        scratch_shapes=[pltpu.VMEM((N_KV_HEADS, BLOCK + tm, LANES), BF16),
                        pltpu.VMEM((N_KV_HEADS, BLOCK + tm, LANES), BF16),
                        pltpu.VMEM((POOL_TOP + tm, POOL_WIDTH), F32),
                        pltpu.VMEM((POOL_TOP + tm, POOL_WIDTH), F32),
                        pltpu.VMEM((POOL_TOP + tm, POOL_WIDTH - POOL_GROUP_DIM), F32),
                        pltpu.VMEM((POOL_TOP + tm, POOL_WIDTH - 2 * POOL_GROUP_DIM), F32),
                        pltpu.VMEM((tm, ATTN_WIDTH), F32),
                        pltpu.VMEM((tm, POOL_WIDTH), F32),
                        pltpu.VMEM((tm, d), BF16)],
        compiler_params=pltpu.CompilerParams(
            dimension_semantics=("arbitrary", "arbitrary"), vmem_limit_bytes=VMEM_LIMIT),
        name="mix",
    )(sinks, x, *mixer_in, xg, wq, kv, kv, wo)


def kernel(x, mem, positions, ffn1_norm, ffn1_w_gate, ffn1_w_up, ffn1_w_down, mix_norm, w_in,
           attn_sinks, pool_w, pool_scale, attn_out_norm, pool_out_norm, w_out, xattn_norm,
           mem_norm, xattn_wq, xattn_wkv, xattn_wo, ffn2_norm, ffn2_w_gate, ffn2_w_up,
           ffn2_w_down, final_norm):
    batch, seq, d = x.shape
    depth = w_in.shape[0]
    assert seq % ROW_TILE == 0 and ROW_TILE % BLOCK == 0 and d == D_MODEL
    n = batch * seq
    pool_w2 = pool_w.reshape(depth, POOL_WIDTH, POOL_GROUP_DIM)
    ffn1_w = (ffn1_w_gate, ffn1_w_up, ffn1_w_down)
    ffn2_w = (ffn2_w_gate, ffn2_w_up, ffn2_w_down)
    rest_w = (w_in, pool_w2, w_out, xattn_wq, xattn_wkv, xattn_wo) + ffn2_w

    xf = x.reshape(n, d)
    fg = final_norm.reshape(1, d)
    (cos_t, sin_t), w1 = _rope_tables(positions, casts=[(w, 0) for w in ffn1_w])
    for l in range(depth):
        xf, wl = _ffn(xf, ffn1_norm, l, *w1, fg, final=False, casts=[(w, l) for w in rest_w])
        win_b, pw_b, wout_b, wq_b, wkv_b, wo_b = wl[:6]
        kv = _mem_kv(mem, mem_norm, l, wkv_b)
        xf = _mix(xf, cos_t, sin_t, batch, seq, l, attn_sinks, mix_norm, win_b, pw_b,
                  pool_scale, attn_out_norm, pool_out_norm, wout_b, xattn_norm, wq_b, kv, wo_b)
        last = l == depth - 1
        xf, w1 = _ffn(xf, ffn2_norm, l, *wl[6:], fg, final=last,
                      casts=[] if last else [(w, l + 1) for w in ffn1_w])
    return xf.reshape(batch, seq, d)
```

```python
import functools

import jax
import jax.numpy as jnp
from jax import lax
from jax.experimental import pallas as pl
from jax.experimental.pallas import tpu as pltpu

F32 = jnp.float32
BF16 = jnp.bfloat16

D_MODEL = 1024
HEAD_DIM = 64
N_Q_HEADS = 8
N_KV_HEADS = 2
ATTN_WIDTH = N_Q_HEADS * HEAD_DIM
KV_WIDTH = N_KV_HEADS * HEAD_DIM
BLOCK = 128
POOL_WINDOWS = (2, 4, 8, 16)
POOL_GROUP_DIM = 128
POOL_WIDTH = 512
POOL_HALO = 16
POOL_TOP = 32
X_HEADS = 4
X_HEAD_DIM = 256
D_FF = 2816
FFN_RES = 0.5
EPS = 1e-6
ROPE_THETA = 10000.0
NEG = -1e30
LOG2_E = 1.4426950408889634
LANES = 128
BF16_SUBLANES = 16

ROW_TILE = 1024
MXU_TILE = 256
FF_CHUNK = 6 * MXU_TILE
FIRST_DOT_ROW_SPLIT = 4
FF_CHUNKS = tuple((lo, min(lo + FF_CHUNK, D_FF)) for lo in range(0, D_FF, FF_CHUNK))
VMEM_LIMIT = 56 * 1024 * 1024


def _rms(x, g):
    ms = jnp.mean(x * x, axis=-1, keepdims=True)
    return x * lax.rsqrt(ms + EPS) * g


def _row_split_dot(a, w):
    rb = a.shape[0] // FIRST_DOT_ROW_SPLIT
    return jnp.concatenate(
        [jnp.dot(a[r * rb:(r + 1) * rb], w, preferred_element_type=F32)
         for r in range(FIRST_DOT_ROW_SPLIT)], axis=0)


def _resident(shape):
    nd = len(shape)
    return pl.BlockSpec(shape, lambda *_: (0,) * nd, pipeline_mode=pl.Buffered(1))


def _rope_table_kernel(pos_ref, freq_ref, *refs, n_casts):
    cast_in, (cos_ref, sin_ref, *cast_out) = refs[:n_casts], refs[n_casts:]
    nfreq = HEAD_DIM // 2
    ngroups = LANES // nfreq
    pos = pos_ref[...].astype(F32)
    q = pos.shape[0] * LANES // ngroups
    pos_t = jnp.concatenate([pos, jnp.zeros((LANES - pos.shape[0], LANES), F32)], axis=0).T
    freq = freq_ref[...]
    group_s = lax.broadcasted_iota(jnp.int32, (LANES, LANES), 1) // nfreq
    slabs = []
    for a in range(q // LANES):
        ang_a = pos_t[:, a:a + 1] * freq
        for k in range(1, ngroups):
            c = k * (q // LANES) + a
            ang_a = jnp.where(group_s == k, pos_t[:, c:c + 1] * freq, ang_a)
        slabs.append(ang_a)
    ang = jnp.concatenate(slabs, axis=0)
    group = lax.broadcasted_iota(jnp.int32, (q, LANES), 1) // nfreq
    even = group % 2 == 0
    for t, ref, rotate_half_sign in ((jnp.cos(ang), cos_ref, False), (jnp.sin(ang), sin_ref, True)):
        pairs = (jnp.where(even, t, pltpu.roll(t, nfreq, 1)),
                 jnp.where(even, pltpu.roll(t, LANES - nfreq, 1), t))
        swaps = tuple(pltpu.roll(b, 2 * nfreq, 1) for b in pairs)
        for k in range(LANES // nfreq):
            b, swapped = pairs[k % 2], swaps[k % 2]
            full = jnp.where(group < 2, b, swapped) if k < 2 else jnp.where(group < 2, swapped, b)
            if rotate_half_sign:
                full = jnp.where(even, -full, full)
            ref[k * q:(k + 1) * q, :] = full
    _cast_job(cast_in, cast_out)


def _rope_tables(positions, casts=()):
    n = positions.size
    inv_freq = ROPE_THETA ** (-jnp.arange(0, HEAD_DIM, 2, dtype=F32) / HEAD_DIM)
    freq = jnp.tile(inv_freq, LANES // (HEAD_DIM // 2)).reshape(1, LANES)
    tm = 2048
    c_in, c_out, c_shape, c_args = _cast_specs(casts, n // tm)
    res = pl.pallas_call(
        functools.partial(_rope_table_kernel, n_casts=len(casts)),
        grid=(n // tm,),
        in_specs=[pl.BlockSpec((tm // LANES, LANES), lambda i: (i, 0)),
                  pl.BlockSpec((1, LANES), lambda i: (0, 0))] + c_in,
        out_specs=[pl.BlockSpec((tm, LANES), lambda i: (i, 0))] * 2 + c_out,
        out_shape=[jax.ShapeDtypeStruct((n, LANES), F32)] * 2 + c_shape,
        compiler_params=pltpu.CompilerParams(dimension_semantics=("arbitrary",)),
        name="rope_tables",
    )(positions.reshape(n // LANES, LANES), freq, *c_args)
    return res[:2], res[2:]


def _cast_blocks(rows, nsteps):
    rb = -(-rows // nsteps)
    rb = -(-rb // BF16_SUBLANES) * BF16_SUBLANES
    while rows % rb:
        rb += BF16_SUBLANES
    return rb, rows // rb


def _cast_specs(casts, nsteps):
    in_specs, out_specs, out_shape, args = [], [], [], []
    for src, src_layer in casts:
        rows, cols = src.shape[1:]
        rb, nb = _cast_blocks(rows, nsteps)
        in_specs.append(pl.BlockSpec(
            (None, rb, cols), lambda i, sl=src_layer, nb=nb: (sl, jnp.minimum(i, nb - 1), 0)))
        out_specs.append(pl.BlockSpec((rb, cols), lambda i, nb=nb: (jnp.minimum(i, nb - 1), 0)))
        out_shape.append(jax.ShapeDtypeStruct((rows, cols), BF16))
        args.append(src)
    return in_specs, out_specs, out_shape, args


def _cast_job(cast_in, cast_out):
    for src, dst in zip(cast_in, cast_out):
        dst[...] = src[...].astype(BF16)


def _ffn_body(x, g, wg_ref, wu_ref, wd_ref):
    xn = _rms(x, g).astype(BF16)
    y = None
    for lo, hi in FF_CHUNKS:
        if lo == 0:
            gate = _row_split_dot(xn, wg_ref[:, lo:hi])
        else:
            gate = jnp.dot(xn, wg_ref[:, lo:hi], preferred_element_type=F32)
        up = jnp.dot(xn, wu_ref[:, lo:hi], preferred_element_type=F32)
        h = (gate / (1.0 + jnp.exp(-gate)) * up).astype(BF16)
        part = jnp.dot(h, wd_ref[lo:hi, :], preferred_element_type=F32)
        y = part if y is None else y + part
    return x + FFN_RES * y


def _ffn_kernel(*refs, layer, final, n_casts):
    x_ref, g_ref, wg_ref, wu_ref, wd_ref, fg_ref = refs[:6]
    cast_in, (o_ref, *cast_out) = refs[6:6 + n_casts], refs[6 + n_casts:]
    out = _ffn_body(x_ref[...], g_ref[layer:layer + 1, :], wg_ref, wu_ref, wd_ref)
    if final:
        out = _rms(out, fg_ref[...])
    o_ref[...] = out
    _cast_job(cast_in, cast_out)


def _ffn(x, g, layer, wg, wu, wd, fg, final, casts=()):
    n, d = x.shape
    tm = ROW_TILE
    nsteps = n // tm
    c_in, c_out, c_shape, c_args = _cast_specs(casts, nsteps)
    in_specs = [pl.BlockSpec((tm, d), lambda i: (i, 0)),
                _resident(g.shape), _resident(wg.shape), _resident(wu.shape),
                _resident(wd.shape), _resident((1, d))] + c_in
    out_specs = [pl.BlockSpec((tm, d), lambda i: (i, 0))] + c_out
    out_shape = [jax.ShapeDtypeStruct((n, d), F32)] + c_shape
    args = [x, g, wg, wu, wd, fg] + c_args
    res = pl.pallas_call(
        functools.partial(_ffn_kernel, layer=layer, final=final, n_casts=len(casts)),
        grid=(nsteps,),
        in_specs=in_specs,
        out_specs=out_specs,
        out_shape=out_shape,
        compiler_params=pltpu.CompilerParams(
            dimension_semantics=("arbitrary",), vmem_limit_bytes=VMEM_LIMIT),
        name="ffn_final" if final else "ffn",
    )(*args)
    return res[0], res[1:]


def _rope(t, cos, sin_signed):
    lane = lax.broadcasted_iota(jnp.int32, t.shape, 1)
    first_half = (lane % HEAD_DIM) < (HEAD_DIM // 2)
    swapped = jnp.where(first_half,
                        pltpu.roll(t, LANES - HEAD_DIM // 2, 1),
                        pltpu.roll(t, HEAD_DIM // 2, 1))
    return t * cos + swapped * sin_signed


def _mixer_body(x_ref, j, layer, sinks_ref, cos_ref, sin_ref, g_ref, win_ref, pw_ref, ps_ref,
                an_ref, pn_ref, wout_ref, kd, vd, ub, l1, l2, l3, oa_ref, ob_ref):
    tm = x_ref.shape[0]
    nblk = tm // BLOCK
    group = N_Q_HEADS // N_KV_HEADS

    @pl.when(j == 0)
    def _():
        kd[:, 0:BLOCK, :] = jnp.zeros((N_KV_HEADS, BLOCK, LANES), BF16)
        vd[:, 0:BLOCK, :] = jnp.zeros((N_KV_HEADS, BLOCK, LANES), BF16)
        ub[0:POOL_TOP, :] = jnp.zeros((POOL_TOP, POOL_WIDTH), F32)

    x = x_ref[...]
    lrow = slice(layer, layer + 1)
    h = _rms(x, g_ref[lrow, :]).astype(BF16)
    proj = _row_split_dot(h, win_ref[...])
    cos = cos_ref[...]
    sin = sin_ref[...]

    low_t = lax.broadcasted_iota(jnp.int32, (tm, LANES), 1) < HEAD_DIM
    k = _rope(proj[:, ATTN_WIDTH:ATTN_WIDTH + KV_WIDTH], cos, sin)
    v = proj[:, ATTN_WIDTH + KV_WIDTH:ATTN_WIDTH + 2 * KV_WIDTH]
    kr = pltpu.roll(k, HEAD_DIM, 1)
    vr = pltpu.roll(v, HEAD_DIM, 1)
    kd[0, BLOCK:BLOCK + tm, :] = jnp.where(low_t, k, kr).astype(BF16)
    kd[1, BLOCK:BLOCK + tm, :] = jnp.where(low_t, kr, k).astype(BF16)
    vd[0, BLOCK:BLOCK + tm, :] = jnp.where(low_t, v, vr).astype(BF16)
    vd[1, BLOCK:BLOCK + tm, :] = jnp.where(low_t, vr, v).astype(BF16)
    ub[POOL_TOP:POOL_TOP + tm, :] = proj[:, ATTN_WIDTH + 2 * KV_WIDTH:]

    low = lax.broadcasted_iota(jnp.int32, (BLOCK, LANES), 1) < HEAD_DIM
    tri_t = (lax.broadcasted_iota(jnp.int32, (BLOCK, BLOCK), 0)
             <= lax.broadcasted_iota(jnp.int32, (BLOCK, BLOCK), 1))
    tri_tb = tri_t.astype(BF16)
    first = j == 0
    scale = HEAD_DIM ** -0.5 * LOG2_E
    zero_bf = jnp.zeros((BLOCK, LANES), BF16)

    q_pairs = [(_rope(proj[:, p * LANES:(p + 1) * LANES], cos, sin) * scale).astype(BF16)
               for p in range(ATTN_WIDTH // LANES)]
    def scores(kvh, b):
        rows = slice(b * BLOCK, (b + 1) * BLOCK)
        pieces = []
        for idx in range(group):
            qb = q_pairs[(kvh * group + idx) // 2][rows, :]
            pieces.append(jnp.where(low if idx % 2 == 0 else ~low, qb, zero_bf))
        q4 = jnp.concatenate(pieces, axis=0)
        return lax.dot_general(kd[kvh, b * BLOCK:(b + 2) * BLOCK, :], q4, (((1,), (1,)), ((), ())),
                               preferred_element_type=F32)

    def attend(kvh, b, s4):
        rows = slice(b * BLOCK, (b + 1) * BLOCK)
        probs = []
        for idx in range(group):
            hcols = slice(idx * BLOCK, (idx + 1) * BLOCK)
            s_prev = s4[0:BLOCK, hcols]
            if b == 0:
                s_prev = jnp.where(first, NEG, s_prev)
            s = jnp.where(tri_t, s4[BLOCK:2 * BLOCK, hcols], s_prev)
            sink = sinks_ref[layer, kvh * group + idx] * LOG2_E
            m = jnp.maximum(jnp.max(s, axis=0, keepdims=True), sink)
            ex = jnp.exp2(s - m)
            den = jnp.sum(ex, axis=0, keepdims=True) + jnp.exp2(sink - m)
            pr = (ex * (1.0 / den)).astype(BF16)
            p_cur = pr * tri_tb
            probs.append(jnp.concatenate([pr - p_cur, p_cur], axis=0))
        p4 = jnp.concatenate(probs, axis=1)
        o4 = lax.dot_general(p4, vd[kvh, b * BLOCK:(b + 2) * BLOCK, :], (((0,), (0,)), ((), ())),
                             preferred_element_type=F32)
        for pp in range(group // 2):
            pcol = (kvh * (group // 2) + pp) * LANES
            oa_ref[rows, pcol:pcol + LANES] = jnp.where(
                low, o4[(2 * pp) * BLOCK:(2 * pp + 1) * BLOCK, :],
                o4[(2 * pp + 1) * BLOCK:(2 * pp + 2) * BLOCK, :])

    units = [(kvh, b) for kvh in range(N_KV_HEADS) for b in range(nblk)]
    s_next = scores(*units[0])
    for u, unit in enumerate(units):
        s_cur = s_next
        if u + 1 < len(units):
            s_next = scores(*units[u + 1])
        attend(*unit, s_cur)

    n1 = tm + POOL_TOP - 8
    l1[8:8 + n1, :] = ub[8:8 + n1, :] + ub[7:7 + n1, :]
    n2 = tm + POOL_TOP - 16
    c2 = POOL_GROUP_DIM
    l2[16:16 + n2, :] = l1[16:16 + n2, c2:] + l1[14:14 + n2, c2:]
    n3 = tm + POOL_TOP - 24
    l3[24:24 + n3, :] = l2[24:24 + n3, c2:] + l2[20:20 + n3, c2:]
    top = POOL_TOP
    sums = [l1[top:top + tm, 0:c2], l2[top:top + tm, 0:c2], l3[top:top + tm, 0:c2],
            l3[top:top + tm, c2:] + l3[top - 8:top - 8 + tm, c2:]]
    row = lax.broadcasted_iota(jnp.int32, (POOL_HALO, POOL_GROUP_DIM), 0) + j * tm
    pooled = []
    for gi, w in enumerate(POOL_WINDOWS):
        cols = slice(gi * POOL_GROUP_DIM, (gi + 1) * POOL_GROUP_DIM)
        cnt = jnp.minimum(row + 1, w).astype(F32)
        mean = jnp.concatenate([sums[gi][0:POOL_HALO] / cnt, sums[gi][POOL_HALO:] * (1.0 / w)],
                               axis=0)
        pooled.append((mean - ub[top:top + tm, cols]).astype(BF16))
    zero_w = jnp.zeros((POOL_GROUP_DIM, POOL_GROUP_DIM), BF16)
    for gp in range(0, len(POOL_WINDOWS), 2):
        ca = slice(gp * POOL_GROUP_DIM, (gp + 1) * POOL_GROUP_DIM)
        cb = slice((gp + 1) * POOL_GROUP_DIM, (gp + 2) * POOL_GROUP_DIM)
        w_pair = jnp.concatenate(
            [jnp.concatenate([pw_ref[ca, :], zero_w], axis=1),
             jnp.concatenate([zero_w, pw_ref[cb, :]], axis=1)], axis=0)
        mixed = jnp.dot(jnp.concatenate(pooled[gp:gp + 2], axis=1), w_pair,
                        preferred_element_type=F32)
        cab = slice(gp * POOL_GROUP_DIM, (gp + 2) * POOL_GROUP_DIM)
        ob_ref[:, cab] = mixed * ps_ref[lrow, cab]

    kd[:, 0:BLOCK, :] = kd[:, tm:tm + BLOCK, :]
    vd[:, 0:BLOCK, :] = vd[:, tm:tm + BLOCK, :]
    ub[POOL_TOP - POOL_HALO:POOL_TOP, :] = ub[tm + POOL_TOP - POOL_HALO:tm + POOL_TOP, :]

    na = _rms(oa_ref[...], an_ref[lrow, :]).astype(BF16)
    nb = _rms(ob_ref[...], pn_ref[lrow, :]).astype(BF16)
    y = _row_split_dot(na, wout_ref[0:ATTN_WIDTH, :])
    y = y + _row_split_dot(nb, wout_ref[ATTN_WIDTH:, :])
    return x + y


def _mem_kv_kernel(mem_ref, g_ref, wkv_ref, kv_ref, *, layer):
    mn = _rms(mem_ref[...], g_ref[layer:layer + 1, :]).astype(BF16)
    kv_ref[...] = jnp.dot(mn, wkv_ref[...], preferred_element_type=F32).astype(BF16)


def _mem_kv(mem, g, layer, wkv):
    b, m, d = mem.shape
    return pl.pallas_call(
        functools.partial(_mem_kv_kernel, layer=layer),
        grid=(b,),
        in_specs=[pl.BlockSpec((None, m, d), lambda i: (i, 0, 0)),
                  _resident(g.shape), _resident(wkv.shape)],
        out_specs=pl.BlockSpec((None, m, 2 * d), lambda i: (i, 0, 0)),
        out_shape=jax.ShapeDtypeStruct((b, m, 2 * d), BF16),
        compiler_params=pltpu.CompilerParams(
            dimension_semantics=("arbitrary",), vmem_limit_bytes=VMEM_LIMIT),
        name="mem_kv",
    )(mem, g, wkv)


def _xattn_body(x, g, wq_ref, k_ref, v_ref, wo_ref, att_ref):
    h = _rms(x, g).astype(BF16)
    q = jnp.dot(h, wq_ref[...], preferred_element_type=F32) * (X_HEAD_DIM ** -0.5 * LOG2_E)
    q = q.astype(BF16)

    def head_cols(hd):
        return slice(hd * X_HEAD_DIM, (hd + 1) * X_HEAD_DIM)

    def scores(hd):
        return lax.dot_general(k_ref[:, head_cols(hd)], q[:, head_cols(hd)],
                               (((1,), (1,)), ((), ())), preferred_element_type=F32)

    s_next = scores(0)
    for hd in range(X_HEADS):
        s = s_next
        if hd + 1 < X_HEADS:
            s_next = scores(hd + 1)
        m = jnp.max(s, axis=0, keepdims=True)
        ex = jnp.exp2(s - m)
        pr = (ex * (1.0 / jnp.sum(ex, axis=0, keepdims=True))).astype(BF16)
        att_ref[:, head_cols(hd)] = lax.dot_general(
            pr, v_ref[:, head_cols(hd)], (((0,), (0,)), ((), ())),
            preferred_element_type=F32).astype(BF16)
    return x + jnp.dot(att_ref[...], wo_ref[...], preferred_element_type=F32)


N_MIXER_IN = 9


def _mix_kernel(sinks_ref, x_ref, *refs, layer):
    mixer_in, (xg_ref, wq_ref, k_ref, v_ref, wo_ref, o_ref, *scratch) = (
        refs[:N_MIXER_IN], refs[N_MIXER_IN:])
    *mixer_scratch, att_ref = scratch
    x = _mixer_body(x_ref, pl.program_id(1), layer, sinks_ref, *mixer_in, *mixer_scratch)
    o_ref[...] = _xattn_body(x, xg_ref[layer:layer + 1, :], wq_ref, k_ref, v_ref, wo_ref, att_ref)


def _mix(x, cos_t, sin_t, batch, seq, layer, sinks, g, win, pw, ps, an, pn, wout, xg, wq, kv, wo):
    n, d = x.shape
    tm = ROW_TILE
    spt = seq // tm
    m = kv.shape[1]
    row_map = lambda b, j: (b * spt + j, 0)
    mixer_in = (cos_t, sin_t, g, win, pw, ps, an, pn, wout)
    assert len(mixer_in) == N_MIXER_IN
    return pl.pallas_call(
        functools.partial(_mix_kernel, layer=layer),
        grid=(batch, spt),
        in_specs=[pl.BlockSpec(memory_space=pltpu.SMEM),
                  pl.BlockSpec((tm, d), row_map),
                  pl.BlockSpec((tm, LANES), row_map),
                  pl.BlockSpec((tm, LANES), row_map),
                  _resident(g.shape), _resident(win.shape), _resident(pw.shape),
                  _resident(ps.shape), _resident(an.shape),
                  _resident(pn.shape), _resident(wout.shape),
                  _resident(xg.shape), _resident(wq.shape),
                  pl.BlockSpec((None, m, d), lambda b, j: (b, 0, 0)),
                  pl.BlockSpec((None, m, d), lambda b, j: (b, 0, 1)),
                  _resident(wo.shape)],
        out_specs=pl.BlockSpec((tm, d), row_map),
        out_shape=jax.ShapeDtypeStruct((n, d), F32),
        scratch_shapes=[pltpu.VMEM((N_KV_HEADS, BLOCK + tm, LANES), BF16),
                        pltpu.VMEM((N_KV_HEADS, BLOCK + tm, LANES), BF16),
                        pltpu.VMEM((POOL_TOP + tm, POOL_WIDTH), F32),
                        pltpu.VMEM((POOL_TOP + tm, POOL_WIDTH), F32),
                        pltpu.VMEM((POOL_TOP + tm, POOL_WIDTH - POOL_GROUP_DIM), F32),
                        pltpu.VMEM((POOL_TOP + tm, POOL_WIDTH - 2 * POOL_GROUP_DIM), F32),
                        pltpu.VMEM((tm, ATTN_WIDTH), F32),
                        pltpu.VMEM((tm, POOL_WIDTH), F32),
                        pltpu.VMEM((tm, d), BF16)],
        compiler_params=pltpu.CompilerParams(
            dimension_semantics=("arbitrary", "arbitrary"), vmem_limit_bytes=VMEM_LIMIT),
        name="mix",
    )(sinks, x, *mixer_in, xg, wq, kv, kv, wo)


def kernel(x, mem, positions, ffn1_norm, ffn1_w_gate, ffn1_w_up, ffn1_w_down, mix_norm, w_in,
           attn_sinks, pool_w, pool_scale, attn_out_norm, pool_out_norm, w_out, xattn_norm,
           mem_norm, xattn_wq, xattn_wkv, xattn_wo, ffn2_norm, ffn2_w_gate, ffn2_w_up,
           ffn2_w_down, final_norm):
    batch, seq, d = x.shape
    depth = w_in.shape[0]
    assert seq % ROW_TILE == 0 and ROW_TILE % BLOCK == 0 and d == D_MODEL
    n = batch * seq
    pool_w2 = pool_w.reshape(depth, POOL_WIDTH, POOL_GROUP_DIM)
    ffn1_w = (ffn1_w_gate, ffn1_w_up, ffn1_w_down)
    ffn2_w = (ffn2_w_gate, ffn2_w_up, ffn2_w_down)
    rest_w = (w_in, pool_w2, w_out, xattn_wq, xattn_wkv, xattn_wo) + ffn2_w

    xf = x.reshape(n, d)
    fg = final_norm.reshape(1, d)
    (cos_t, sin_t), w1 = _rope_tables(positions, casts=[(w, 0) for w in ffn1_w])
    for l in range(depth):
        xf, wl = _ffn(xf, ffn1_norm, l, *w1, fg, final=False, casts=[(w, l) for w in rest_w])
        win_b, pw_b, wout_b, wq_b, wkv_b, wo_b = wl[:6]
        kv = _mem_kv(mem, mem_norm, l, wkv_b)
        xf = _mix(xf, cos_t, sin_t, batch, seq, l, attn_sinks, mix_norm, win_b, pw_b,
                  pool_scale, attn_out_norm, pool_out_norm, wout_b, xattn_norm, wq_b, kv, wo_b)
        last = l == depth - 1
        xf, w1 = _ffn(xf, ffn2_norm, l, *wl[6:], fg, final=last,
                      casts=[] if last else [(w, l + 1) for w in ffn1_w])
    return xf.reshape(batch, seq, d)
```

```python
import functools

import jax
import jax.numpy as jnp
from jax import lax
from jax.experimental import pallas as pl
from jax.experimental.pallas import tpu as pltpu

F32 = jnp.float32
BF16 = jnp.bfloat16

D_MODEL = 1024
HEAD_DIM = 64
N_Q_HEADS = 8
N_KV_HEADS = 2
ATTN_WIDTH = N_Q_HEADS * HEAD_DIM
KV_WIDTH = N_KV_HEADS * HEAD_DIM
BLOCK = 128
POOL_WINDOWS = (2, 4, 8, 16)
POOL_GROUP_DIM = 128
POOL_WIDTH = 512
POOL_HALO = 16
POOL_TOP = 32
X_HEADS = 4
X_HEAD_DIM = 256
D_FF = 2816
FFN_RES = 0.5
EPS = 1e-6
ROPE_THETA = 10000.0
NEG = -1e30
LOG2_E = 1.4426950408889634
LANES = 128
BF16_SUBLANES = 16

ROW_TILE = 1024
MXU_TILE = 256
FF_CHUNK = 6 * MXU_TILE
FIRST_DOT_ROW_SPLIT = 4
FF_CHUNKS = tuple((lo, min(lo + FF_CHUNK, D_FF)) for lo in range(0, D_FF, FF_CHUNK))
VMEM_LIMIT = 56 * 1024 * 1024


def _rms(x, g):
    ms = jnp.mean(x * x, axis=-1, keepdims=True)
    return x * lax.rsqrt(ms + EPS) * g


def _row_split_dot(a, w):
    rb = a.shape[0] // FIRST_DOT_ROW_SPLIT
    return jnp.concatenate(
        [jnp.dot(a[r * rb:(r + 1) * rb], w, preferred_element_type=F32)
         for r in range(FIRST_DOT_ROW_SPLIT)], axis=0)


def _resident(shape):
    nd = len(shape)
    return pl.BlockSpec(shape, lambda *_: (0,) * nd, pipeline_mode=pl.Buffered(1))


def _rope_table_kernel(pos_ref, freq_ref, *refs, n_casts):
    cast_in, (cos_ref, sin_ref, *cast_out) = refs[:n_casts], refs[n_casts:]
    nfreq = HEAD_DIM // 2
    ngroups = LANES // nfreq
    pos = pos_ref[...].astype(F32)
    q = pos.shape[0] * LANES // ngroups
    pos_t = jnp.concatenate([pos, jnp.zeros((LANES - pos.shape[0], LANES), F32)], axis=0).T
    freq = freq_ref[...]
    group_s = lax.broadcasted_iota(jnp.int32, (LANES, LANES), 1) // nfreq
    slabs = []
    for a in range(q // LANES):
        ang_a = pos_t[:, a:a + 1] * freq
        for k in range(1, ngroups):
            c = k * (q // LANES) + a
            ang_a = jnp.where(group_s == k, pos_t[:, c:c + 1] * freq, ang_a)
        slabs.append(ang_a)
    ang = jnp.concatenate(slabs, axis=0)
    group = lax.broadcasted_iota(jnp.int32, (q, LANES), 1) // nfreq
    even = group % 2 == 0
    for t, ref, rotate_half_sign in ((jnp.cos(ang), cos_ref, False), (jnp.sin(ang), sin_ref, True)):
        pairs = (jnp.where(even, t, pltpu.roll(t, nfreq, 1)),
                 jnp.where(even, pltpu.roll(t, LANES - nfreq, 1), t))
        swaps = tuple(pltpu.roll(b, 2 * nfreq, 1) for b in pairs)
        for k in range(LANES // nfreq):
            b, swapped = pairs[k % 2], swaps[k % 2]
            full = jnp.where(group < 2, b, swapped) if k < 2 else jnp.where(group < 2, swapped, b)
            if rotate_half_sign:
                full = jnp.where(even, -full, full)
            ref[k * q:(k + 1) * q, :] = full
    _cast_job(cast_in, cast_out)


def _rope_tables(positions, casts=()):
    n = positions.size
    inv_freq = ROPE_THETA ** (-jnp.arange(0, HEAD_DIM, 2, dtype=F32) / HEAD_DIM)
    freq = jnp.tile(inv_freq, LANES // (HEAD_DIM // 2)).reshape(1, LANES)
    tm = 2048
    c_in, c_out, c_shape, c_args = _cast_specs(casts, n // tm)
    res = pl.pallas_call(
        functools.partial(_rope_table_kernel, n_casts=len(casts)),
        grid=(n // tm,),
        in_specs=[pl.BlockSpec((tm // LANES, LANES), lambda i: (i, 0)),
                  pl.BlockSpec((1, LANES), lambda i: (0, 0))] + c_in,
        out_specs=[pl.BlockSpec((tm, LANES), lambda i: (i, 0))] * 2 + c_out,
        out_shape=[jax.ShapeDtypeStruct((n, LANES), F32)] * 2 + c_shape,
        compiler_params=pltpu.CompilerParams(dimension_semantics=("arbitrary",)),
        name="rope_tables",
    )(positions.reshape(n // LANES, LANES), freq, *c_args)
    return res[:2], res[2:]


def _cast_blocks(rows, nsteps):
    rb = -(-rows // nsteps)
    rb = -(-rb // BF16_SUBLANES) * BF16_SUBLANES
    while rows % rb:
        rb += BF16_SUBLANES
    return rb, rows // rb


def _cast_specs(casts, nsteps):
    in_specs, out_specs, out_shape, args = [], [], [], []
    for src, src_layer in casts:
        rows, cols = src.shape[1:]
        rb, nb = _cast_blocks(rows, nsteps)
        in_specs.append(pl.BlockSpec(
            (None, rb, cols), lambda i, sl=src_layer, nb=nb: (sl, jnp.minimum(i, nb - 1), 0)))
        out_specs.append(pl.BlockSpec((rb, cols), lambda i, nb=nb: (jnp.minimum(i, nb - 1), 0)))
        out_shape.append(jax.ShapeDtypeStruct((rows, cols), BF16))
        args.append(src)
    return in_specs, out_specs, out_shape, args


def _cast_job(cast_in, cast_out):
    for src, dst in zip(cast_in, cast_out):
        dst[...] = src[...].astype(BF16)


def _ffn_body(x, g, wg_ref, wu_ref, wd_ref):
    xn = _rms(x, g).astype(BF16)
    y = None
    for lo, hi in FF_CHUNKS:
        if lo == 0:
            gate = _row_split_dot(xn, wg_ref[:, lo:hi])
        else:
            gate = jnp.dot(xn, wg_ref[:, lo:hi], preferred_element_type=F32)
        up = jnp.dot(xn, wu_ref[:, lo:hi], preferred_element_type=F32)
        h = (gate / (1.0 + jnp.exp(-gate)) * up).astype(BF16)
        part = jnp.dot(h, wd_ref[lo:hi, :], preferred_element_type=F32)
        y = part if y is None else y + part
    return x + FFN_RES * y


def _ffn_kernel(*refs, layer, final, n_casts):
    x_ref, g_ref, wg_ref, wu_ref, wd_ref, fg_ref = refs[:6]
    cast_in, (o_ref, *cast_out) = refs[6:6 + n_casts], refs[6 + n_casts:]
    out = _ffn_body(x_ref[...], g_ref[layer:layer + 1, :], wg_ref, wu_ref, wd_ref)
    if final:
        out = _rms(out, fg_ref[...])
    o_ref[...] = out
    _cast_job(cast_in, cast_out)


def _ffn(x, g, layer, wg, wu, wd, fg, final, casts=()):
    n, d = x.shape
    tm = ROW_TILE
    nsteps = n // tm
    c_in, c_out, c_shape, c_args = _cast_specs(casts, nsteps)
    in_specs = [pl.BlockSpec((tm, d), lambda i: (i, 0)),
                _resident(g.shape), _resident(wg.shape), _resident(wu.shape),
                _resident(wd.shape), _resident((1, d))] + c_in
    out_specs = [pl.BlockSpec((tm, d), lambda i: (i, 0))] + c_out
    out_shape = [jax.ShapeDtypeStruct((n, d), F32)] + c_shape
    args = [x, g, wg, wu, wd, fg] + c_args
    res = pl.pallas_call(
        functools.partial(_ffn_kernel, layer=layer, final=final, n_casts=len(casts)),
        grid=(nsteps,),
        in_specs=in_specs,
        out_specs=out_specs,
        out_shape=out_shape,
        compiler_params=pltpu.CompilerParams(
            dimension_semantics=("arbitrary",), vmem_limit_bytes=VMEM_LIMIT),
        name="ffn_final" if final else "ffn",
    )(*args)
    return res[0], res[1:]


def _rope(t, cos, sin_signed):
    lane = lax.broadcasted_iota(jnp.int32, t.shape, 1)
    first_half = (lane % HEAD_DIM) < (HEAD_DIM // 2)
    swapped = jnp.where(first_half,
                        pltpu.roll(t, LANES - HEAD_DIM // 2, 1),
                        pltpu.roll(t, HEAD_DIM // 2, 1))
    return t * cos + swapped * sin_signed


def _mixer_body(x_ref, j, layer, sinks_ref, cos_ref, sin_ref, g_ref, win_ref, pw_ref, ps_ref,
                an_ref, pn_ref, wout_ref, kd, vd, ub, l1, l2, l3, oa_ref, ob_ref):
    tm = x_ref.shape[0]
    nblk = tm // BLOCK
    group = N_Q_HEADS // N_KV_HEADS

    @pl.when(j == 0)
    def _():
        kd[:, 0:BLOCK, :] = jnp.zeros((N_KV_HEADS, BLOCK, LANES), BF16)
        vd[:, 0:BLOCK, :] = jnp.zeros((N_KV_HEADS, BLOCK, LANES), BF16)
        ub[0:POOL_TOP, :] = jnp.zeros((POOL_TOP, POOL_WIDTH), F32)

    x = x_ref[...]
    lrow = slice(layer, layer + 1)
    h = _rms(x, g_ref[lrow, :]).astype(BF16)
    rq = tm // FIRST_DOT_ROW_SPLIT
    low_t = lax.broadcasted_iota(jnp.int32, (rq, LANES), 1) < HEAD_DIM
    scale = HEAD_DIM ** -0.5 * LOG2_E

    def project(r):
        rows = slice(r * rq, (r + 1) * rq)
        proj = jnp.dot(h[rows], win_ref[...], preferred_element_type=F32)
        cos = cos_ref[rows, :]
        sin = sin_ref[rows, :]
        k = _rope(proj[:, ATTN_WIDTH:ATTN_WIDTH + KV_WIDTH], cos, sin)
        v = proj[:, ATTN_WIDTH + KV_WIDTH:ATTN_WIDTH + 2 * KV_WIDTH]
        kr = pltpu.roll(k, HEAD_DIM, 1)
        vr = pltpu.roll(v, HEAD_DIM, 1)
        krows = slice(BLOCK + r * rq, BLOCK + (r + 1) * rq)
        kd[0, krows, :] = jnp.where(low_t, k, kr).astype(BF16)
        kd[1, krows, :] = jnp.where(low_t, kr, k).astype(BF16)
        vd[0, krows, :] = jnp.where(low_t, v, vr).astype(BF16)
        vd[1, krows, :] = jnp.where(low_t, vr, v).astype(BF16)
        ub[POOL_TOP + r * rq:POOL_TOP + (r + 1) * rq, :] = proj[:, ATTN_WIDTH + 2 * KV_WIDTH:]
        return [(_rope(proj[:, p * LANES:(p + 1) * LANES], cos, sin) * scale).astype(BF16)
                for p in range(ATTN_WIDTH // LANES)]

    low = lax.broadcasted_iota(jnp.int32, (BLOCK, LANES), 1) < HEAD_DIM
    tri_t = (lax.broadcasted_iota(jnp.int32, (BLOCK, BLOCK), 0)
             <= lax.broadcasted_iota(jnp.int32, (BLOCK, BLOCK), 1))
    tri_tb = tri_t.astype(BF16)
    first = j == 0
    zero_bf = jnp.zeros((BLOCK, LANES), BF16)
    bpq = rq // BLOCK
    q_blocks = {}

    def scores(kvh, b):
        rows = slice((b % bpq) * BLOCK, (b % bpq + 1) * BLOCK)
        pieces = []
        for idx in range(group):
            qb = q_blocks[b // bpq][(kvh * group + idx) // 2][rows, :]
            pieces.append(jnp.where(low if idx % 2 == 0 else ~low, qb, zero_bf))
        q4 = jnp.concatenate(pieces, axis=0)
        return lax.dot_general(kd[kvh, b * BLOCK:(b + 2) * BLOCK, :], q4, (((1,), (1,)), ((), ())),
                               preferred_element_type=F32)

    def attend(kvh, b, s4):
        rows = slice(b * BLOCK, (b + 1) * BLOCK)
        probs = []
        for idx in range(group):
            hcols = slice(idx * BLOCK, (idx + 1) * BLOCK)
            s_prev = s4[0:BLOCK, hcols]
            if b == 0:
                s_prev = jnp.where(first, NEG, s_prev)
            s = jnp.where(tri_t, s4[BLOCK:2 * BLOCK, hcols], s_prev)
            sink = sinks_ref[layer, kvh * group + idx] * LOG2_E
            m = jnp.maximum(jnp.max(s, axis=0, keepdims=True), sink)
            ex = jnp.exp2(s - m)
            den = jnp.sum(ex, axis=0, keepdims=True) + jnp.exp2(sink - m)
            pr = (ex * (1.0 / den)).astype(BF16)
            p_cur = pr * tri_tb
            probs.append(jnp.concatenate([pr - p_cur, p_cur], axis=0))
        p4 = jnp.concatenate(probs, axis=1)
        o4 = lax.dot_general(p4, vd[kvh, b * BLOCK:(b + 2) * BLOCK, :], (((0,), (0,)), ((), ())),
                             preferred_element_type=F32)
        for pp in range(group // 2):
            pcol = (kvh * (group // 2) + pp) * LANES
            oa_ref[rows, pcol:pcol + LANES] = jnp.where(
                low, o4[(2 * pp) * BLOCK:(2 * pp + 1) * BLOCK, :],
                o4[(2 * pp + 1) * BLOCK:(2 * pp + 2) * BLOCK, :])

    units = [(kvh, b) for r in range(FIRST_DOT_ROW_SPLIT) for kvh in range(N_KV_HEADS)
             for b in range(r * bpq, (r + 1) * bpq)]
    q_blocks[0] = project(0)
    s_next = scores(*units[0])
    for u, (kvh, b) in enumerate(units):
        r = b // bpq
        if r + 1 < FIRST_DOT_ROW_SPLIT and r + 1 not in q_blocks:
            q_blocks[r + 1] = project(r + 1)
        s_cur = s_next
        if u + 1 < len(units):
            s_next = scores(*units[u + 1])
        attend(kvh, b, s_cur)

    n1 = tm + POOL_TOP - 8
    l1[8:8 + n1, :] = ub[8:8 + n1, :] + ub[7:7 + n1, :]
    n2 = tm + POOL_TOP - 16
    c2 = POOL_GROUP_DIM
    l2[16:16 + n2, :] = l1[16:16 + n2, c2:] + l1[14:14 + n2, c2:]
    n3 = tm + POOL_TOP - 24
    l3[24:24 + n3, :] = l2[24:24 + n3, c2:] + l2[20:20 + n3, c2:]
    top = POOL_TOP
    sums = [l1[top:top + tm, 0:c2], l2[top:top + tm, 0:c2], l3[top:top + tm, 0:c2],
            l3[top:top + tm, c2:] + l3[top - 8:top - 8 + tm, c2:]]
    row = lax.broadcasted_iota(jnp.int32, (POOL_HALO, POOL_GROUP_DIM), 0) + j * tm
    pooled = []
    for gi, w in enumerate(POOL_WINDOWS):
        cols = slice(gi * POOL_GROUP_DIM, (gi + 1) * POOL_GROUP_DIM)
        cnt = jnp.minimum(row + 1, w).astype(F32)
        mean = jnp.concatenate([sums[gi][0:POOL_HALO] / cnt, sums[gi][POOL_HALO:] * (1.0 / w)],
                               axis=0)
        pooled.append((mean - ub[top:top + tm, cols]).astype(BF16))
    zero_w = jnp.zeros((POOL_GROUP_DIM, POOL_GROUP_DIM), BF16)
    for gp in range(0, len(POOL_WINDOWS), 2):
        ca = slice(gp * POOL_GROUP_DIM, (gp + 1) * POOL_GROUP_DIM)
        cb = slice((gp + 1) * POOL_GROUP_DIM, (gp + 2) * POOL_GROUP_DIM)
        w_pair = jnp.concatenate(
            [jnp.concatenate([pw_ref[ca, :], zero_w], axis=1),
             jnp.concatenate([zero_w, pw_ref[cb, :]], axis=1)], axis=0)
        mixed = jnp.dot(jnp.concatenate(pooled[gp:gp + 2], axis=1), w_pair,
                        preferred_element_type=F32)
        cab = slice(gp * POOL_GROUP_DIM, (gp + 2) * POOL_GROUP_DIM)
        ob_ref[:, cab] = mixed * ps_ref[lrow, cab]

    kd[:, 0:BLOCK, :] = kd[:, tm:tm + BLOCK, :]
    vd[:, 0:BLOCK, :] = vd[:, tm:tm + BLOCK, :]
    ub[POOL_TOP - POOL_HALO:POOL_TOP, :] = ub[tm + POOL_TOP - POOL_HALO:tm + POOL_TOP, :]

    na = _rms(oa_ref[...], an_ref[lrow, :]).astype(BF16)
    nb = _rms(ob_ref[...], pn_ref[lrow, :]).astype(BF16)
    y = _row_split_dot(na, wout_ref[0:ATTN_WIDTH, :])
    y = y + _row_split_dot(nb, wout_ref[ATTN_WIDTH:, :])
    return x + y


def _mem_kv_kernel(mem_ref, g_ref, wkv_ref, kv_ref, *, layer):
    mn = _rms(mem_ref[...], g_ref[layer:layer + 1, :]).astype(BF16)
    kv_ref[...] = jnp.dot(mn, wkv_ref[...], preferred_element_type=F32).astype(BF16)


def _mem_kv(mem, g, layer, wkv):
    b, m, d = mem.shape
    return pl.pallas_call(
        functools.partial(_mem_kv_kernel, layer=layer),
        grid=(b,),
        in_specs=[pl.BlockSpec((None, m, d), lambda i: (i, 0, 0)),
                  _resident(g.shape), _resident(wkv.shape)],
        out_specs=pl.BlockSpec((None, m, 2 * d), lambda i: (i, 0, 0)),
        out_shape=jax.ShapeDtypeStruct((b, m, 2 * d), BF16),
        compiler_params=pltpu.CompilerParams(
            dimension_semantics=("arbitrary",), vmem_limit_bytes=VMEM_LIMIT),
        name="mem_kv",
    )(mem, g, wkv)


def _xattn_body(x, g, wq_ref, k_ref, v_ref, wo_ref, att_ref):
    h = _rms(x, g).astype(BF16)
    q = jnp.dot(h, wq_ref[...], preferred_element_type=F32) * (X_HEAD_DIM ** -0.5 * LOG2_E)
    q = q.astype(BF16)

    def head_cols(hd):
        return slice(hd * X_HEAD_DIM, (hd + 1) * X_HEAD_DIM)

    def scores(hd):
        return lax.dot_general(k_ref[:, head_cols(hd)], q[:, head_cols(hd)],
                               (((1,), (1,)), ((), ())), preferred_element_type=F32)

    s_next = scores(0)
    for hd in range(X_HEADS):
        s = s_next
        if hd + 1 < X_HEADS:
            s_next = scores(hd + 1)
        m = jnp.max(s, axis=0, keepdims=True)
        ex = jnp.exp2(s - m)
        pr = (ex * (1.0 / jnp.sum(ex, axis=0, keepdims=True))).astype(BF16)
        att_ref[:, head_cols(hd)] = lax.dot_general(
            pr, v_ref[:, head_cols(hd)], (((0,), (0,)), ((), ())),
            preferred_element_type=F32).astype(BF16)
    return x + jnp.dot(att_ref[...], wo_ref[...], preferred_element_type=F32)


N_MIXER_IN = 9


def _mix_kernel(sinks_ref, x_ref, *refs, layer):
    mixer_in, (xg_ref, wq_ref, k_ref, v_ref, wo_ref, o_ref, *scratch) = (
        refs[:N_MIXER_IN], refs[N_MIXER_IN:])
    *mixer_scratch, att_ref = scratch
    x = _mixer_body(x_ref, pl.program_id(1), layer, sinks_ref, *mixer_in, *mixer_scratch)
    o_ref[...] = _xattn_body(x, xg_ref[layer:layer + 1, :], wq_ref, k_ref, v_ref, wo_ref, att_ref)


def _mix(x, cos_t, sin_t, batch, seq, layer, sinks, g, win, pw, ps, an, pn, wout, xg, wq, kv, wo):
    n, d = x.shape
    tm = ROW_TILE
    spt = seq // tm
    m = kv.shape[1]
    row_map = lambda b, j: (b * spt + j, 0)
    mixer_in = (cos_t, sin_t, g, win, pw, ps, an, pn, wout)
    assert len(mixer_in) == N_MIXER_IN
    return pl.pallas_call(
        functools.partial(_mix_kernel, layer=layer),
        grid=(batch, spt),
        in_specs=[pl.BlockSpec(memory_space=pltpu.SMEM),
                  pl.BlockSpec((tm, d), row_map),
                  pl.BlockSpec((tm, LANES), row_map),
                  pl.BlockSpec((tm, LANES), row_map),
                  _resident(g.shape), _resident(win.shape), _resident(pw.shape),
                  _resident(ps.shape), _resident(an.shape),
                  _resident(pn.shape), _resident(wout.shape),
                  _resident(xg.shape), _resident(wq.shape),
                  pl.BlockSpec((None, m, d), lambda b, j: (b, 0, 0)),
                  pl.BlockSpec((None, m, d), lambda b, j: (b, 0, 1)),
                  _resident(wo.shape)],
        out_specs=pl.BlockSpec((tm, d), row_map),
        out_shape=jax.ShapeDtypeStruct((n, d), F32),
        scratch_shapes=[pltpu.VMEM((N_KV_HEADS, BLOCK + tm, LANES), BF16),
                        pltpu.VMEM((N_KV_HEADS, BLOCK + tm, LANES), BF16),
                        pltpu.VMEM((POOL_TOP + tm, POOL_WIDTH), F32),
                        pltpu.VMEM((POOL_TOP + tm, POOL_WIDTH), F32),
                        pltpu.VMEM((POOL_TOP + tm, POOL_WIDTH - POOL_GROUP_DIM), F32),
                        pltpu.VMEM((POOL_TOP + tm, POOL_WIDTH - 2 * POOL_GROUP_DIM), F32),
                        pltpu.VMEM((tm, ATTN_WIDTH), F32),
                        pltpu.VMEM((tm, POOL_WIDTH), F32),
                        pltpu.VMEM((tm, d), BF16)],
        compiler_params=pltpu.CompilerParams(
            dimension_semantics=("arbitrary", "arbitrary"), vmem_limit_bytes=VMEM_LIMIT),
        name="mix",
    )(sinks, x, *mixer_in, xg, wq, kv, kv, wo)


def kernel(x, mem, positions, ffn1_norm, ffn1_w_gate, ffn1_w_up, ffn1_w_down, mix_norm, w_in,
           attn_sinks, pool_w, pool_scale, attn_out_norm, pool_out_norm, w_out, xattn_norm,
           mem_norm, xattn_wq, xattn_wkv, xattn_wo, ffn2_norm, ffn2_w_gate, ffn2_w_up,
           ffn2_w_down, final_norm):
    batch, seq, d = x.shape
    depth = w_in.shape[0]
    assert seq % ROW_TILE == 0 and ROW_TILE % BLOCK == 0 and d == D_MODEL
    n = batch * seq
    pool_w2 = pool_w.reshape(depth, POOL_WIDTH, POOL_GROUP_DIM)
    ffn1_w = (ffn1_w_gate, ffn1_w_up, ffn1_w_down)
    ffn2_w = (ffn2_w_gate, ffn2_w_up, ffn2_w_down)
    rest_w = (w_in, pool_w2, w_out, xattn_wq, xattn_wkv, xattn_wo) + ffn2_w

    xf = x.reshape(n, d)
    fg = final_norm.reshape(1, d)
    (cos_t, sin_t), w1 = _rope_tables(positions, casts=[(w, 0) for w in ffn1_w])
    for l in range(depth):
        xf, wl = _ffn(xf, ffn1_norm, l, *w1, fg, final=False, casts=[(w, l) for w in rest_w])
        win_b, pw_b, wout_b, wq_b, wkv_b, wo_b = wl[:6]
        kv = _mem_kv(mem, mem_norm, l, wkv_b)
        xf = _mix(xf, cos_t, sin_t, batch, seq, l, attn_sinks, mix_norm, win_b, pw_b,
                  pool_scale, attn_out_norm, pool_out_norm, wout_b, xattn_norm, wq_b, kv, wo_b)
        last = l == depth - 1
        xf, w1 = _ffn(xf, ffn2_norm, l, *wl[6:], fg, final=last,
                      casts=[] if last else [(w, l + 1) for w in ffn1_w])
    return xf.reshape(batch, seq, d)
```
